```python
import math
import jax
import jax.numpy as jnp
from jax import lax
import numpy as np

D_MODEL = 4096
BATCH = 4
SEQ = 4096
DEPTH = 1

CTX_LEN = 256
GRID_W = 64
N_DIR = 2
CHUNK = 128
EPS = 1e-6

M_HEADS = 8
M_QK_DIM = D_MODEL // 16
M_V_DIM = D_MODEL // 8
M_QK = M_HEADS * M_QK_DIM
M_V = M_HEADS * M_V_DIM
M_GATES = N_DIR * 2 * M_HEADS

S_INNER = D_MODEL
S_HEAD_DIM = 64
S_HEADS = S_INNER // S_HEAD_DIM
S_GROUPS = 8
S_HPG = S_HEADS // S_GROUPS
S_STATE = 128
S_CONV = 4
S_BC = S_GROUPS * S_STATE
S_CONV_CH = S_INNER + 2 * S_BC
S_DT = N_DIR * S_HEADS

D_FF = -(-8 * D_MODEL // (3 * 256)) * 256

STATE_SPLITS = (M_QK, M_QK, M_V, M_GATES, S_CONV_CH, S_DT)
OUT_SPLITS = (M_V, S_INNER, D_MODEL, D_MODEL)
N_STATE_COLS = sum(STATE_SPLITS)
N_IN = N_STATE_COLS + sum(OUT_SPLITS)

kernel_name = "hybrid_mlstm_ssd_prefix_block"


def split_cols(a, sizes):
    idx = [int(i) for i in np.cumsum(sizes)[:-1]]
    return jnp.split(a, idx, axis=-1)


def rms_norm(x, g):
    xf = x.astype(jnp.float32)
    y = xf * lax.rsqrt(jnp.mean(xf * xf, axis=-1, keepdims=True) + EPS)
    return (y * g.astype(jnp.float32)).astype(x.dtype)


def modulate(h, shift, scale):
    return h * (1 + scale) + shift


def to_chunks(a, axis):
    t = a.shape[axis]
    a = a.reshape(a.shape[:axis] + (t // CHUNK, CHUNK) + a.shape[axis + 1:])
    return jnp.moveaxis(a, axis, 0)


def from_chunks(a, axis):
    a = jnp.moveaxis(a, 0, axis)
    return a.reshape(a.shape[:axis] + (-1,) + a.shape[axis + 2:])


def grid_transpose(a, rows, cols):
    b = a.shape[0]
    return a.reshape((b, rows, cols) + a.shape[2:]).swapaxes(1, 2).reshape(a.shape)


def mlstm_scan(q, k, v, log_i, log_f, state, emit):
    causal = jnp.tril(jnp.ones((CHUNK, CHUNK), dtype=bool))

    def step(carry, inp):
        C, n, m = carry
        qc, kc, vc, lic, lfc = inp
        b = jnp.cumsum(lfc, axis=-1)
        logd = jnp.where(causal, b[..., :, None] - b[..., None, :] + lic[..., None, :], -jnp.inf)
        m_carry = b + m[..., None]
        m_row = jnp.maximum(m_carry, jnp.max(logd, axis=-1))
        d_mat = jnp.exp(logd - m_row[..., None])
        w_carry = jnp.exp(m_carry - m_row)
        s = jnp.einsum('bhjd,bhsd->bhjs', qc, kc) * d_mat
        num = jnp.einsum('bhjs,bhsv->bhjv', s, vc) + w_carry[..., None] * jnp.einsum('bhvd,bhjd->bhjv', C, qc)
        den = jnp.sum(s, axis=-1) + w_carry * jnp.einsum('bhd,bhjd->bhj', n, qc)
        h = num / jnp.maximum(jnp.abs(den), jnp.exp(-m_row))[..., None]
        m_end = m_row[..., -1]
        w_state = jnp.exp(b[..., -1:] - b + lic - m_end[..., None])
        decay = jnp.exp(b[..., -1] + m - m_end)
        C = decay[..., None, None] * C + jnp.einsum('bhs,bhsv,bhsd->bhvd', w_state, vc, kc)
        n = decay[..., None] * n + jnp.einsum('bhs,bhsd->bhd', w_state, kc)
        return (C, n, m_end), (h if emit else None)

    chunks = tuple(to_chunks(a, 2) for a in (q, k, v, log_i, log_f))
    state, h = lax.scan(step, state, chunks)
    return state, (from_chunks(h, 2) if emit else None)


def mlstm_heads(q, k, v, gts, gate_b):
    b, t, _ = q.shape
    f32 = jnp.float32
    heads = lambda a, dh: a.astype(f32).reshape(b, t, M_HEADS, dh).transpose(0, 2, 1, 3)
    g = (gts.astype(f32).reshape(b, t, N_DIR, 2, M_HEADS) + gate_b.astype(f32)).transpose(2, 3, 0, 4, 1)
    return (heads(q, M_QK_DIM) * M_QK_DIM ** -0.5, heads(k, M_QK_DIM), heads(v, M_V_DIM),
            g[:, 0], jax.nn.log_sigmoid(g[:, 1]))


def mlstm_out(h, o, head_g):
    b, nh, t, dv = h.shape
    hn = h * lax.rsqrt(jnp.mean(h * h, axis=-1, keepdims=True) + EPS) * head_g.astype(jnp.float32).reshape(nh, 1, dv)
    hn = hn.transpose(0, 2, 1, 3).reshape(b, t, nh * dv)
    return (hn * jax.nn.sigmoid(o.astype(jnp.float32))).astype(o.dtype)


def mlstm_mixer(q, k, v, gts, o, qc, kc, vc, gtsc, oc, gate_b, head_g, emit_ctx):
    lat = mlstm_heads(q, k, v, gts, gate_b)
    ctx = mlstm_heads(qc, kc, vc, gtsc, gate_b)
    b = q.shape[0]
    f32 = jnp.float32
    init = (jnp.zeros((b, M_HEADS, M_V_DIM, M_QK_DIM), f32), jnp.zeros((b, M_HEADS, M_QK_DIM), f32),
            jnp.zeros((b, M_HEADS), f32))
    h_lat, h_ctx = 0.0, 0.0
    for d in range(N_DIR):
        f = (lambda a: jnp.flip(a, axis=2)) if d == 1 else (lambda a: a)
        st, hc = mlstm_scan(f(ctx[0]), f(ctx[1]), f(ctx[2]), f(ctx[3][d]), f(ctx[4][d]), init, emit_ctx)
        _, hl = mlstm_scan(f(lat[0]), f(lat[1]), f(lat[2]), f(lat[3][d]), f(lat[4][d]), st, True)
        h_lat = h_lat + f(hl)
        if emit_ctx:
            h_ctx = h_ctx + f(hc)
    y_lat = mlstm_out(h_lat, o, head_g)
    y_ctx = mlstm_out(h_ctx, oc, head_g) if emit_ctx else None
    return y_lat, y_ctx


def dw_conv(a, w, bias):
    out = lax.conv_general_dilated(a, w[:, None, :].astype(a.dtype), window_strides=(1,), padding='SAME',
                                   dimension_numbers=('NWC', 'WIO', 'NWC'), feature_group_count=a.shape[-1])
    return jax.nn.silu(out + bias.astype(a.dtype))


def ssd_scan(xs, dt, la, bm, cm, state, emit):
    causal = jnp.tril(jnp.ones((CHUNK, CHUNK), dtype=bool))[None, :, :, None, None]

    def step(h, inp):
        xc, dtc, lac, bc, cc = inp
        cum = jnp.cumsum(lac, axis=1)
        seg = jnp.where(causal, cum[:, :, None] - cum[:, None, :], -jnp.inf)
        w = jnp.einsum('bjgn,bsgn->bjsg', cc, bc)[..., None] * jnp.exp(seg) * dtc[:, None]
        y = (jnp.einsum('bjsgr,bsgrp->bjgrp', w, xc)
             + jnp.exp(cum)[..., None] * jnp.einsum('bgrpn,bjgn->bjgrp', h, cc))
        w_state = jnp.exp(cum[:, -1:] - cum) * dtc
        h = jnp.exp(cum[:, -1])[..., None, None] * h + jnp.einsum('bsgr,bsgrp,bsgn->bgrpn', w_state, xc, bc)
        return h, (y if emit else None)

    chunks = tuple(to_chunks(a, 1) for a in (xs, dt, la, bm, cm))
    state, y = lax.scan(step, state, chunks)
    return state, (from_chunks(y, 1) if emit else None)


def ssd_heads(xbc, dt_pre, dt_bias, a_log):
    b, t, _ = xbc.shape
    f32 = jnp.float32
    xs, bm, cm = split_cols(xbc.astype(f32), (S_INNER, S_BC, S_BC))
    dt = jax.nn.softplus(dt_pre.astype(f32).reshape(b, t, N_DIR, S_GROUPS, S_HPG)
                         + dt_bias.astype(f32).reshape(N_DIR, S_GROUPS, S_HPG))
    la = -dt * jnp.exp(a_log.astype(f32)).reshape(N_DIR, S_GROUPS, S_HPG)
    return (xs.reshape(b, t, S_GROUPS, S_HPG, S_HEAD_DIM), bm.reshape(b, t, S_GROUPS, S_STATE),
            cm.reshape(b, t, S_GROUPS, S_STATE), dt, la)


def ssd_out(y, z, g):
    b, t, _ = y.shape
    yg = (y * jax.nn.silu(z.astype(jnp.float32))).reshape(b, t, S_GROUPS, S_INNER // S_GROUPS)
    yg = yg * lax.rsqrt(jnp.mean(yg * yg, axis=-1, keepdims=True) + EPS)
    return (yg.reshape(b, t, S_INNER) * g.astype(jnp.float32)).astype(z.dtype)


def ssd_mixer(xbc, dt, z, xbcc, dtc, zc, conv_w, conv_b, dt_bias, a_log, d_skip, norm_g, rows, emit_ctx):
    b, t, _ = xbc.shape
    tc = xbcc.shape[1]
    to_col = lambda a: grid_transpose(a, rows, GRID_W)
    lat = ssd_heads(dw_conv(to_col(xbc), conv_w, conv_b), to_col(dt), dt_bias, a_log)
    ctx = ssd_heads(dw_conv(xbcc, conv_w, conv_b), dtc, dt_bias, a_log)
    init = jnp.zeros((b, S_GROUPS, S_HPG, S_HEAD_DIM, S_STATE), jnp.float32)
    skip = d_skip.astype(jnp.float32).reshape(S_GROUPS, S_HPG, 1)
    y_lat = skip * lat[0]
    y_ctx = skip * ctx[0] if emit_ctx else None
    for d in range(N_DIR):
        f = (lambda a: jnp.flip(a, axis=1)) if d == 1 else (lambda a: a)
        st, yc = ssd_scan(f(ctx[0]), f(ctx[3][:, :, d]), f(ctx[4][:, :, d]), f(ctx[1]), f(ctx[2]), init, emit_ctx)
        _, yl = ssd_scan(f(lat[0]), f(lat[3][:, :, d]), f(lat[4][:, :, d]), f(lat[1]), f(lat[2]), st, True)
        y_lat = y_lat + f(yl)
        if emit_ctx:
            y_ctx = y_ctx + f(yc)
    y_lat = grid_transpose(y_lat.reshape(b, t, S_INNER), GRID_W, rows)
    out_lat = ssd_out(y_lat, z, norm_g)
    out_ctx = ssd_out(y_ctx.reshape(b, tc, S_INNER), zc, norm_g) if emit_ctx else None
    return out_lat, out_ctx


def merge_branches(y_m, y_s, g_m, g_s, w_bm, w_bs, w_out):
    return (jax.nn.sigmoid(g_m) * (y_m @ w_bm) + jax.nn.sigmoid(g_s) * (y_s @ w_bs)) @ w_out


def swiglu(h, wg, wu, wd):
    return (jax.nn.silu(h @ wg) * (h @ wu)) @ wd


def setup_inputs(seed: int = 0) -> dict:
    key = jax.random.key(seed)
    ks = jax.random.split(key, 22)
    f32 = jnp.float32
    nrm = lambda k, shape, scale: jax.random.normal(k, shape, f32) * scale
    x = nrm(ks[0], (BATCH, SEQ, D_MODEL), 1.0)
    c = nrm(ks[1], (BATCH, D_MODEL), 1.0)
    ctx = nrm(ks[2], (BATCH, CTX_LEN, D_MODEL), 1.0)
    c_ctx = nrm(ks[3], (D_MODEL,), 1.0)
    w_ada = nrm(ks[4], (DEPTH, D_MODEL, 6 * D_MODEL), 0.5 * D_MODEL ** -0.5)
    b_ada = nrm(ks[5], (DEPTH, 6 * D_MODEL), 0.02)
    norm_g = 1.0 + nrm(ks[6], (DEPTH, 4, D_MODEL), 0.02)
    w_in = nrm(ks[7], (DEPTH, D_MODEL, N_IN), D_MODEL ** -0.5)
    gate_base = jnp.stack([jnp.zeros((M_HEADS,), f32), jnp.linspace(3.0, 6.0, M_HEADS, dtype=f32)])
    m_gate_b = gate_base + nrm(ks[8], (DEPTH, N_DIR, 2, M_HEADS), 0.1)
    m_norm_g = 1.0 + nrm(ks[9], (DEPTH, M_V), 0.02)
    s_conv_w = nrm(ks[10], (DEPTH, S_CONV, S_CONV_CH), S_CONV ** -0.5)
    s_conv_b = nrm(ks[11], (DEPTH, S_CONV_CH), 0.02)
    dt0 = jnp.exp(jax.random.uniform(ks[12], (DEPTH, N_DIR, S_HEADS), f32, math.log(1e-3), math.log(1e-1)))
    s_dt_bias = dt0 + jnp.log(-jnp.expm1(-dt0))
    s_a_log = jnp.log(jax.random.uniform(ks[13], (DEPTH, N_DIR, S_HEADS), f32, 1.0, 16.0))
    s_d = 1.0 + nrm(ks[14], (DEPTH, S_HEADS), 0.1)
    s_norm_g = 1.0 + nrm(ks[15], (DEPTH, S_INNER), 0.02)
    w_bm = nrm(ks[16], (DEPTH, M_V, D_MODEL), M_V ** -0.5)
    w_bs = nrm(ks[17], (DEPTH, S_INNER, D_MODEL), S_INNER ** -0.5)
    w_out = nrm(ks[18], (DEPTH, D_MODEL, D_MODEL), D_MODEL ** -0.5)
    w_ffn_gate = nrm(ks[19], (DEPTH, D_MODEL, D_FF), D_MODEL ** -0.5)
    w_ffn_up = nrm(ks[20], (DEPTH, D_MODEL, D_FF), D_MODEL ** -0.5)
    w_ffn_down = nrm(ks[21], (DEPTH, D_FF, D_MODEL), D_FF ** -0.5)
    return {"x": x, "c": c, "ctx": ctx, "c_ctx": c_ctx, "w_ada": w_ada, "b_ada": b_ada, "norm_g": norm_g,
            "w_in": w_in, "m_gate_b": m_gate_b, "m_norm_g": m_norm_g, "s_conv_w": s_conv_w,
            "s_conv_b": s_conv_b, "s_dt_bias": s_dt_bias, "s_a_log": s_a_log, "s_d": s_d,
            "s_norm_g": s_norm_g, "w_bm": w_bm, "w_bs": w_bs, "w_out": w_out,
            "w_ffn_gate": w_ffn_gate, "w_ffn_up": w_ffn_up, "w_ffn_down": w_ffn_down}


def reference(x, c, ctx, c_ctx, w_ada, b_ada, norm_g, w_in, m_gate_b, m_norm_g, s_conv_w, s_conv_b,
              s_dt_bias, s_a_log, s_d, s_norm_g, w_bm, w_bs, w_out, w_ffn_gate, w_ffn_up, w_ffn_down):
    rows = x.shape[1] // GRID_W
    xc = ctx
    for li in range(DEPTH):
        emit_ctx = li < DEPTH - 1
        g_n = norm_g[li]
        mod_x = (jax.nn.silu(c) @ w_ada[li] + b_ada[li])[:, None, :]
        mod_c = (jax.nn.silu(c_ctx) @ w_ada[li] + b_ada[li])[None, None, :]
        sh1, sc1, ga1, sh2, sc2, ga2 = jnp.split(mod_x, 6, axis=-1)
        csh1, csc1, cga1, csh2, csc2, cga2 = jnp.split(mod_c, 6, axis=-1)

        u = modulate(rms_norm(x, g_n[0]), sh1, sc1)
        uc = modulate(rms_norm(xc, g_n[0]), csh1, csc1)
        proj = u @ w_in[li]
        proj_c = uc @ (w_in[li] if emit_ctx else w_in[li][:, :N_STATE_COLS])
        q, k, v, gts, xbc, dt = split_cols(proj[..., :N_STATE_COLS], STATE_SPLITS)
        o, z, g_m, g_s = split_cols(proj[..., N_STATE_COLS:], OUT_SPLITS)
        qc, kc, vc, gtsc, xbcc, dtc = split_cols(proj_c[..., :N_STATE_COLS], STATE_SPLITS)
        if emit_ctx:
            oc, zc, g_mc, g_sc = split_cols(proj_c[..., N_STATE_COLS:], OUT_SPLITS)
        else:
            oc, zc, g_mc, g_sc = None, None, None, None

        y_m, y_mc = mlstm_mixer(q, k, v, gts, o, qc, kc, vc, gtsc, oc, m_gate_b[li], m_norm_g[li], emit_ctx)
        y_s, y_sc = ssd_mixer(xbc, dt, z, xbcc, dtc, zc, s_conv_w[li], s_conv_b[li], s_dt_bias[li],
                              s_a_log[li], s_d[li], s_norm_g[li], rows, emit_ctx)

        mix = merge_branches(y_m, y_s, g_m, g_s, w_bm[li], w_bs[li], w_out[li])
        x_new = x + ga1 * rms_norm(mix, g_n[1])
        hf = swiglu(modulate(rms_norm(x_new, g_n[2]), sh2, sc2), w_ffn_gate[li], w_ffn_up[li], w_ffn_down[li])
        x_new = x_new + ga2 * rms_norm(hf, g_n[3])

        if emit_ctx:
            mix_c = merge_branches(y_mc, y_sc, g_mc, g_sc, w_bm[li], w_bs[li], w_out[li])
            xc_new = xc + cga1 * rms_norm(mix_c, g_n[1])
            hfc = swiglu(modulate(rms_norm(xc_new, g_n[2]), csh2, csc2), w_ffn_gate[li], w_ffn_up[li], w_ffn_down[li])
            xc = xc_new + cga2 * rms_norm(hfc, g_n[3])
        x = x_new
    return x
```

```python
import functools

import jax
import jax.numpy as jnp
from jax import lax
from jax.experimental import pallas as pl
from jax.experimental.pallas import tpu as pltpu

F32 = jnp.float32
BF16 = jnp.bfloat16

CHUNK = 128
GRID_W = 64
EPS = 1e-6
N_DIR = 2
M_HEADS = 8
S_GROUPS = 8
S_HEAD_DIM = 64
S_STATE = 128
S_CONV = 4
LANES = 128
MOD_ROWS = 8
VMEM_LIMIT = 56 * 1024 * 1024


def _params(n_axes):
    return pltpu.CompilerParams(dimension_semantics=("arbitrary",) * n_axes,
                                vmem_limit_bytes=VMEM_LIMIT)


def _dot(a, b):
    return jnp.dot(a, b, preferred_element_type=F32)


def _dot_nt(a, b):
    return lax.dot_general(a, b, (((1,), (1,)), ((), ())), preferred_element_type=F32)


def _dot_tn(a, b):
    return lax.dot_general(a, b, (((0,), (0,)), ((), ())), preferred_element_type=F32)


def _cumsum_time(tri, x):
    hi = x.astype(BF16)
    r1 = x - hi.astype(F32)
    mid = r1.astype(BF16)
    lo = (r1 - mid.astype(F32)).astype(BF16)
    return _dot(tri, hi) + _dot(tri, mid) + _dot(tri, lo)


def _sigmoid(x):
    return 1.0 / (1.0 + jnp.exp(-x))


def _silu(x):
    return x * _sigmoid(x)


def _softplus(x):
    return jnp.maximum(x, 0.0) + jnp.log(1.0 + jnp.exp(-jnp.abs(x)))


def _time_masks(reverse):
    jj = lax.broadcasted_iota(jnp.int32, (CHUNK, CHUNK), 0)
    ss = lax.broadcasted_iota(jnp.int32, (CHUNK, CHUNK), 1)
    mask = (ss >= jj) if reverse else (ss <= jj)
    return mask, jnp.where(mask, 1.0, 0.0).astype(BF16)


def _mm_kernel(a_ref, w_ref, o_ref):
    o_ref[...] = _dot(a_ref[...].astype(BF16), w_ref[...].astype(BF16)).astype(o_ref.dtype)


def _matmul(a, w, out_dtype, tm, tn, row_off=0, rows=None, name="matmul"):
    k = a.shape[1]
    n = w.shape[1]
    rows = a.shape[0] - row_off if rows is None else rows
    assert rows % tm == 0 and n % tn == 0 and row_off % tm == 0, (rows, n, tm, tn, row_off)
    off = row_off // tm
    return pl.pallas_call(
        _mm_kernel,
        grid=(rows // tm, n // tn),
        in_specs=[pl.BlockSpec((tm, k), lambda i, j: (i + off, 0)),
                  pl.BlockSpec((k, tn), lambda i, j: (0, j))],
        out_specs=pl.BlockSpec((tm, tn), lambda i, j: (i, j)),
        out_shape=jax.ShapeDtypeStruct((rows, n), out_dtype),
        compiler_params=_params(2), name=name)(a, w)


def _ada_kernel(c_ref, w_ref, b_ref, o_ref):
    a = _silu(c_ref[...]).astype(BF16)
    o_ref[...] = _dot(a, w_ref[...].astype(BF16)) + b_ref[...]


def _ada(cvec, w, b, tn):
    k, n = w.shape
    return pl.pallas_call(
        _ada_kernel,
        grid=(n // tn,),
        in_specs=[pl.BlockSpec((MOD_ROWS, k), lambda j: (0, 0)),
                  pl.BlockSpec((k, tn), lambda j: (0, j)),
                  pl.BlockSpec((1, tn), lambda j: (0, j))],
        out_specs=pl.BlockSpec((MOD_ROWS, tn), lambda j: (0, j)),
        out_shape=jax.ShapeDtypeStruct((MOD_ROWS, n), F32),
        compiler_params=_params(1), name="ada")(cvec, w, b)


def _merge_kernel(ym_ref, ys_ref, wm_ref, ws_ref, gm_ref, gs_ref, o_ref):
    am = _dot(ym_ref[...], wm_ref[...])
    a_s = _dot(ys_ref[...], ws_ref[...])
    o_ref[...] = (_sigmoid(gm_ref[...].astype(F32)) * am
                  + _sigmoid(gs_ref[...].astype(F32)) * a_s).astype(o_ref.dtype)


def _merge(ym, ys, wm, ws, gm, gs, tm, tn):
    m, k = ym.shape
    n = wm.shape[1]
    a_spec = pl.BlockSpec((tm, k), lambda i, j: (i, 0))
    w_spec = pl.BlockSpec((k, tn), lambda i, j: (0, j))
    o_spec = pl.BlockSpec((tm, tn), lambda i, j: (i, j))
    return pl.pallas_call(
        _merge_kernel,
        grid=(m // tm, n // tn),
        in_specs=[a_spec, a_spec, w_spec, w_spec, o_spec, o_spec],
        out_specs=o_spec,
        out_shape=jax.ShapeDtypeStruct((m, n), BF16),
        compiler_params=_params(2), name="merge")(ym, ys, wm, ws, gm, gs)


def _swiglu_kernel(a_ref, wg_ref, wu_ref, o_ref):
    a = a_ref[...]
    o_ref[...] = (_silu(_dot(a, wg_ref[...])) * _dot(a, wu_ref[...])).astype(o_ref.dtype)


def _swiglu_up(a, wg, wu, tm, tn):
    m, k = a.shape
    n = wg.shape[1]
    w_spec = pl.BlockSpec((k, tn), lambda i, j: (0, j))
    return pl.pallas_call(
        _swiglu_kernel,
        grid=(m // tm, n // tn),
        in_specs=[pl.BlockSpec((tm, k), lambda i, j: (i, 0)), w_spec, w_spec],
        out_specs=pl.BlockSpec((tm, tn), lambda i, j: (i, j)),
        out_shape=jax.ShapeDtypeStruct((m, n), BF16),
        compiler_params=_params(2), name="swiglu_up")(a, wg, wu)


def _rms(x, g):
    return x * lax.rsqrt(jnp.mean(x * x, axis=-1, keepdims=True) + EPS) * g


def _norm_mod_kernel(xc_ref, xl_ref, g_ref, sh_ref, sc_ref, o_ref, *, n_ctx_blocks):
    i = pl.program_id(0)

    def emit(x):
        o_ref[...] = (_rms(x, g_ref[...]) * (1.0 + sc_ref[0]) + sh_ref[0]).astype(o_ref.dtype)

    @pl.when(i < n_ctx_blocks)
    def _():
        emit(xc_ref[...])

    @pl.when(i >= n_ctx_blocks)
    def _():
        emit(xl_ref[...])


def _norm_mod(xc, xl, g, mod3, batch, tm):
    d = xl.shape[1]
    ncb = xc.shape[0] // tm
    nlb = xl.shape[0] // tm
    per_batch = nlb // batch

    def mod_row(i):
        return jnp.where(i < ncb, batch, jnp.maximum(i - ncb, 0) // per_batch)

    return pl.pallas_call(
        functools.partial(_norm_mod_kernel, n_ctx_blocks=ncb),
        grid=(ncb + nlb,),
        in_specs=[pl.BlockSpec((tm, d), lambda i: (jnp.minimum(i, ncb - 1), 0)),
                  pl.BlockSpec((tm, d), lambda i: (jnp.maximum(i - ncb, 0), 0)),
                  pl.BlockSpec((1, d), lambda i: (0, 0)),
                  pl.BlockSpec((1, 1, d), lambda i: (mod_row(i) * 6 + 0, 0, 0)),
                  pl.BlockSpec((1, 1, d), lambda i: (mod_row(i) * 6 + 1, 0, 0))],
        out_specs=pl.BlockSpec((tm, d), lambda i: (i, 0)),
        out_shape=jax.ShapeDtypeStruct(((ncb + nlb) * tm, d), BF16),
        compiler_params=_params(1), name="norm_mod")(xc, xl, g, mod3, mod3)


def _resid_mid_kernel(mix_ref, x_ref, g1_ref, g2_ref, ga_ref, sh_ref, sc_ref, xn_ref, h_ref):
    xn = x_ref[...] + ga_ref[0] * _rms(mix_ref[...].astype(F32), g1_ref[...])
    xn_ref[...] = xn
    h_ref[...] = (_rms(xn, g2_ref[...]) * (1.0 + sc_ref[0]) + sh_ref[0]).astype(h_ref.dtype)


def _resid_mid(mix, x, g1, g2, mod3, batch, tm):
    m, d = x.shape
    per_batch = m // tm // batch
    row = pl.BlockSpec((tm, d), lambda i: (i, 0))
    vec = pl.BlockSpec((1, d), lambda i: (0, 0))

    def mod(which):
        return pl.BlockSpec((1, 1, d), lambda i: ((i // per_batch) * 6 + which, 0, 0))

    return pl.pallas_call(
        _resid_mid_kernel,
        grid=(m // tm,),
        in_specs=[row, row, vec, vec, mod(2), mod(3), mod(4)],
        out_specs=[row, row],
        out_shape=[jax.ShapeDtypeStruct((m, d), F32), jax.ShapeDtypeStruct((m, d), BF16)],
        compiler_params=_params(1), name="resid_mid")(mix, x, g1, g2, mod3, mod3, mod3)


def _resid_out_kernel(hf_ref, x_ref, g_ref, ga_ref, o_ref):
    o_ref[...] = x_ref[...] + ga_ref[0] * _rms(hf_ref[...].astype(F32), g_ref[...])


def _resid_out(hf, x, g, mod3, batch, tm):
    m, d = x.shape
    per_batch = m // tm // batch
    row = pl.BlockSpec((tm, d), lambda i: (i, 0))
    return pl.pallas_call(
        _resid_out_kernel,
        grid=(m // tm,),
        in_specs=[row, row, pl.BlockSpec((1, d), lambda i: (0, 0)),
                  pl.BlockSpec((1, 1, d), lambda i: ((i // per_batch) * 6 + 5, 0, 0))],
        out_specs=row,
        out_shape=jax.ShapeDtypeStruct((m, d), F32),
        compiler_params=_params(1), name="resid_out")(hf, x, g, mod3)


def _mlstm_kernel(*refs, reverse, d, ncc, final, dk, dv):
    if final:
        (q_ref, k_ref, v_ref, g_ref, gb_ref, hb_ref, og_ref, hg_ref,
         o_ref, ct_sc, n_sc, m_sc) = refs
    else:
        q_ref, k_ref, v_ref, g_ref, gb_ref, o_ref, ct_sc, n_sc, m_sc = refs
    t = pl.program_id(1)

    @pl.when(t == 0)
    def _():
        ct_sc[...] = jnp.zeros_like(ct_sc)
        n_sc[...] = jnp.zeros_like(n_sc)
        m_sc[...] = jnp.zeros_like(m_sc)

    mask, tri = _time_masks(reverse)
    last = 0 if reverse else CHUNK - 1

    g = g_ref[...] + gb_ref[...]
    log_f = jnp.minimum(g, 0.0) - jnp.log(1.0 + jnp.exp(-jnp.abs(g)))
    csum = _cumsum_time(tri, log_f)
    off_i = d * 2 * M_HEADS
    off_f = off_i + M_HEADS
    li_t = g if off_i == 0 else pltpu.roll(g, LANES - off_i, axis=1)
    b_t = pltpu.roll(csum, LANES - off_f, axis=1)
    r_t = (li_t - b_t).T

    for h in range(M_HEADS):
        b_c = b_t[:, h:h + 1]
        li_c = li_t[:, h:h + 1]
        logd = jnp.where(mask, b_c + r_t[h:h + 1, :], -jnp.inf)
        m_prev = m_sc[h:h + 1, 0:1]
        m_carry = b_c + m_prev
        m_row = jnp.maximum(m_carry, jnp.max(logd, axis=1, keepdims=True))
        d_mat = jnp.exp(logd - m_row)
        w_carry = jnp.exp(m_carry - m_row)

        q = q_ref[:, h * dk:(h + 1) * dk]
        k = k_ref[:, h * dk:(h + 1) * dk]
        v = v_ref[:, h * dv:(h + 1) * dv]
        ct = ct_sc[h]
        n_row = n_sc[h:h + 1, :]

        s = _dot_nt(q, k) * d_mat
        num = _dot(s.astype(BF16), v) + w_carry * _dot(q, ct.astype(BF16))
        den = (jnp.sum(s, axis=1, keepdims=True)
               + w_carry * jnp.sum(q.astype(F32) * n_row, axis=1, keepdims=True))
        h_out = num / jnp.maximum(jnp.abs(den), jnp.exp(-m_row))

        m_end = m_row[last:last + 1, :]
        b_end = b_c[last:last + 1, :]
        w_state = jnp.exp(b_end - b_c + li_c - m_end)
        decay = jnp.exp(b_end + m_prev - m_end)
        ct_sc[h] = decay * ct + _dot_tn(k, (w_state * v.astype(F32)).astype(BF16))
        n_sc[h:h + 1, :] = decay * n_row + jnp.sum(w_state * k.astype(F32), axis=0, keepdims=True)
        m_sc[h:h + 1, :] = jnp.broadcast_to(m_end, (1, LANES))

        @pl.when(t >= ncc)
        def _():
            if final:
                ht = h_out + hb_ref[:, h * dv:(h + 1) * dv]
                hn = _rms(ht, hg_ref[:, h * dv:(h + 1) * dv])
                og = _sigmoid(og_ref[:, h * dv:(h + 1) * dv].astype(F32))
                o_ref[:, h * dv:(h + 1) * dv] = (hn * og).astype(o_ref.dtype)
            else:
                o_ref[:, h * dv:(h + 1) * dv] = h_out.astype(o_ref.dtype)


def _mlstm_pass(q, k, v, gates, gate_b, *, batch, ncc, nlc, d, reverse, extra=None):
    dk = q.shape[1] // M_HEADS
    dv = v.shape[1] // M_HEADS
    final = extra is not None

    def comb_idx(b, t):
        cc = (ncc - 1 - t) if reverse else t
        lc = (nlc - 1 - (t - ncc)) if reverse else (t - ncc)
        return jnp.where(t < ncc, b * ncc + cc, batch * ncc + b * nlc + lc)

    def lat_idx(b, t):
        tt = jnp.maximum(t - ncc, 0)
        return b * nlc + ((nlc - 1 - tt) if reverse else tt)

    qk_spec = pl.BlockSpec((CHUNK, M_HEADS * dk), lambda b, t: (comb_idx(b, t), 0))
    v_spec = pl.BlockSpec((CHUNK, M_HEADS * dv), lambda b, t: (comb_idx(b, t), 0))
    lat_spec = pl.BlockSpec((CHUNK, M_HEADS * dv), lambda b, t: (lat_idx(b, t), 0))
    in_specs = [qk_spec, qk_spec, v_spec,
                pl.BlockSpec((CHUNK, LANES), lambda b, t: (comb_idx(b, t), 0)),
                pl.BlockSpec((1, LANES), lambda b, t: (0, 0))]
    args = [q, k, v, gates, gate_b]
    if final:
        in_specs += [lat_spec, lat_spec, pl.BlockSpec((1, M_HEADS * dv), lambda b, t: (0, 0))]
        args += list(extra)
    return pl.pallas_call(
        functools.partial(_mlstm_kernel, reverse=reverse, d=d, ncc=ncc, final=final, dk=dk, dv=dv),
        grid=(batch, ncc + nlc),
        in_specs=in_specs,
        out_specs=lat_spec,
        out_shape=jax.ShapeDtypeStruct((batch * nlc * CHUNK, M_HEADS * dv), BF16 if final else F32),
        scratch_shapes=[pltpu.VMEM((M_HEADS, dk, dv), F32),
                        pltpu.VMEM((M_HEADS, dk), F32),
                        pltpu.VMEM((M_HEADS, LANES), F32)],
        compiler_params=_params(2), name="mlstm_final" if final else "mlstm_bwd")(*args)


def _conv_taps(w_ref, b_ref):
    w = w_ref[...]
    return w[0:1, :], w[1:2, :], w[2:3, :], w[3:4, :], b_ref[...]


def _conv_lat_kernel(x_ref, w_ref, b_ref, o_ref):
    rows, cols = x_ref.shape[1], x_ref.shape[2]
    w0, w1, w2, w3, bias = _conv_taps(w_ref, b_ref)
    cidx = lax.broadcasted_iota(jnp.int32, (cols, 1), 0)

    def ld(r):
        return x_ref[0, r].astype(F32)

    def from_prev_col(a):
        return jnp.where(cidx >= 1, pltpu.roll(a, 1, axis=0), 0.0)

    def from_next_col(a):
        return jnp.where(cidx <= cols - 2, pltpu.roll(a, cols - 1, axis=0), 0.0)

    def emit(r, prev, cur, nxt1, nxt2):
        o_ref[0, r] = _silu(w0 * prev + w1 * cur + w2 * nxt1 + w3 * nxt2 + bias).astype(o_ref.dtype)

    emit(0, from_prev_col(ld(rows - 1)), ld(0), ld(1), ld(2))

    def body(r, carry):
        emit(r, ld(r - 1), ld(r), ld(r + 1), ld(r + 2))
        return carry

    lax.fori_loop(1, rows - 2, body, 0)
    top0 = from_next_col(ld(0))
    emit(rows - 2, ld(rows - 3), ld(rows - 2), ld(rows - 1), top0)
    emit(rows - 1, ld(rows - 2), ld(rows - 1), top0, from_next_col(ld(1)))


def _conv_ctx_kernel(x_ref, w_ref, b_ref, o_ref):
    tlen = x_ref.shape[1]
    w0, w1, w2, w3, bias = _conv_taps(w_ref, b_ref)
    x = x_ref[0].astype(F32)
    tidx = lax.broadcasted_iota(jnp.int32, (tlen, 1), 0)
    prev = jnp.where(tidx >= 1, pltpu.roll(x, 1, axis=0), 0.0)
    nxt1 = jnp.where(tidx <= tlen - 2, pltpu.roll(x, tlen - 1, axis=0), 0.0)
    nxt2 = jnp.where(tidx <= tlen - 3, pltpu.roll(x, tlen - 2, axis=0), 0.0)
    o_ref[0] = _silu(w0 * prev + w1 * x + w2 * nxt1 + w3 * nxt2 + bias).astype(o_ref.dtype)


def _conv(x, w, b, ch_off, ch_n, cb, latent):
    assert ch_off % cb == 0 and ch_n % cb == 0
    off = ch_off // cb
    lead = x.shape[:-1]
    nd = len(lead)
    blk = (1,) + lead[1:] + (cb,)
    zeros = (0,) * (nd - 1)
    return pl.pallas_call(
        _conv_lat_kernel if latent else _conv_ctx_kernel,
        grid=(lead[0], ch_n // cb),
        in_specs=[pl.BlockSpec(blk, lambda b_, j: (b_,) + zeros + (j + off,)),
                  pl.BlockSpec((S_CONV, cb), lambda b_, j: (0, j + off)),
                  pl.BlockSpec((1, cb), lambda b_, j: (0, j + off))],
        out_specs=pl.BlockSpec(blk, lambda b_, j: (b_,) + zeros + (j,)),
        out_shape=jax.ShapeDtypeStruct(lead + (ch_n,), BF16),
        compiler_params=_params(2), name="conv_lat" if latent else "conv_ctx")(x, w, b)


def _expand_heads(tile, first, hpg, lane_lo):
    m = tile.shape[0]
    cols = []
    for p in range(hpg // 2):
        a = jnp.broadcast_to(tile[:, first + 2 * p:first + 2 * p + 1], (m, LANES))
        b = jnp.broadcast_to(tile[:, first + 2 * p + 1:first + 2 * p + 2], (m, LANES))
        cols.append(jnp.where(lane_lo[:m], a, b))
    return cols[0] if len(cols) == 1 else jnp.concatenate(cols, axis=1)


def _ssd_kernel(*refs, reverse, d, ncc, final, hpg, ci):
    if final:
        (xl_ref, xc_ref, bl_ref, bc_ref, cl_ref, cc_ref, dl_ref, dc_ref, dtb_ref, alog_ref,
         yb_ref, z_ref, skip_ref, ng_ref, o_ref, h_sc) = refs
    else:
        (xl_ref, xc_ref, bl_ref, bc_ref, cl_ref, cc_ref, dl_ref, dc_ref, dtb_ref, alog_ref,
         o_ref, h_sc) = refs
    t = pl.program_id(1)
    half = CHUNK // 2
    gw = hpg * S_HEAD_DIM
    sbc = S_GROUPS * S_STATE

    @pl.when(t == 0)
    def _():
        h_sc[...] = jnp.zeros_like(h_sc)

    is_ctx = t < ncc

    def chunk_of(lat_ref, ctx_ref, width, lo=0, n=None):
        n = width if n is None else n
        tile = lat_ref[0]
        lat = jnp.concatenate([tile[:, lo:lo + n], tile[:, width + lo:width + lo + n]], axis=0)
        return jnp.where(is_ctx, ctx_ref[0][:, lo:lo + n], lat)

    mask, tri = _time_masks(reverse)
    last = 0 if reverse else CHUNK - 1
    lane_lo = lax.broadcasted_iota(jnp.int32, (CHUNK, LANES), 1) < S_HEAD_DIM
    lane_head = lax.broadcasted_iota(jnp.int32, (CHUNK, gw), 1) // S_HEAD_DIM

    x_all = chunk_of(xl_ref, xc_ref, ci)
    b_all = chunk_of(bl_ref, bc_ref, sbc)
    c_all = chunk_of(cl_ref, cc_ref, sbc)
    dtp = chunk_of(dl_ref, dc_ref, N_DIR * LANES, d * LANES, LANES) + dtb_ref[...]
    dt = _softplus(dtp)
    cum = _cumsum_time(tri, -dt * jnp.exp(alog_ref[...]))
    cum_t = cum.T
    cum_end = cum[last:last + 1, :]
    e_cum = jnp.exp(cum)
    e_rest = jnp.exp(cum_end - cum)
    e_end = jnp.exp(cum_end)

    for g in range(S_GROUPS):
        first = g * hpg
        xg = x_all[:, g * gw:(g + 1) * gw].astype(F32)
        bg = b_all[:, g * S_STATE:(g + 1) * S_STATE]
        cg = c_all[:, g * S_STATE:(g + 1) * S_STATE]
        cb = _dot_nt(cg, bg)
        ws = []
        for r in range(hpg):
            idx = first + r
            seg = jnp.where(mask, cum[:, idx:idx + 1] - cum_t[idx:idx + 1, :], -jnp.inf)
            ws.append((cb * jnp.exp(seg)).astype(BF16))
        w_cat = jnp.concatenate(ws, axis=1)
        xdt = xg * _expand_heads(dt, first, hpg, lane_lo)
        xdt_b = xdt.astype(BF16)
        bd = jnp.concatenate([jnp.where(lane_head == r, xdt_b, jnp.zeros_like(xdt_b))
                              for r in range(hpg)], axis=0)
        h_t = h_sc[g]
        y = (_dot(w_cat, bd)
             + _expand_heads(e_cum, first, hpg, lane_lo) * _dot(cg, h_t.astype(BF16)))
        wx = (xdt * _expand_heads(e_rest, first, hpg, lane_lo)).astype(BF16)
        h_sc[g] = _expand_heads(e_end, first, hpg, lane_lo) * h_t + _dot_tn(bg, wx)

        @pl.when(t >= ncc)
        def _():
            lo = slice(g * gw, (g + 1) * gw)
            hi = slice(ci + g * gw, ci + (g + 1) * gw)
            if final:
                yb = jnp.concatenate([yb_ref[0, :, lo], yb_ref[0, :, hi]], axis=0)
                zg = jnp.concatenate([z_ref[0, :, lo], z_ref[0, :, hi]], axis=0).astype(F32)
                yt = (y + yb + skip_ref[:, lo] * xg) * _silu(zg)
                out = _rms(yt, ng_ref[:, lo]).astype(o_ref.dtype)
            else:
                out = y.astype(o_ref.dtype)
            o_ref[0, :, lo] = out[:half]
            o_ref[0, :, hi] = out[half:]


def _ssd_pass(lat, ctx, dt_bias, a_log, *, batch, ncc, nlc, d, reverse, hpg, extra=None):
    ci = ctx[0].shape[-1]
    sbc = S_GROUPS * S_STATE
    final = extra is not None
    rows = lat[0].shape[1]
    assert rows * 2 == CHUNK

    def lat_col(b, t):
        tt = jnp.maximum(t - ncc, 0)
        return (b, 0, (nlc - 1 - tt) if reverse else tt)

    def ctx_chunk(b, t):
        tt = jnp.minimum(t, ncc - 1)
        return (b, (ncc - 1 - tt) if reverse else tt, 0)

    def lat_spec(width):
        return pl.BlockSpec((1, rows, 2 * width), lat_col)

    def ctx_spec(width):
        return pl.BlockSpec((1, CHUNK, width), ctx_chunk)

    vec = pl.BlockSpec((1, LANES), lambda b, t: (0, 0))
    in_specs = [lat_spec(ci), ctx_spec(ci), lat_spec(sbc), ctx_spec(sbc), lat_spec(sbc), ctx_spec(sbc),
                lat_spec(N_DIR * LANES), ctx_spec(N_DIR * LANES), vec, vec]
    args = [lat[0], ctx[0], lat[1], ctx[1], lat[2], ctx[2], lat[3], ctx[3], dt_bias, a_log]
    if final:
        chan = pl.BlockSpec((1, ci), lambda b, t: (0, 0))
        in_specs += [lat_spec(ci), lat_spec(ci), chan, chan]
        args += list(extra)
    return pl.pallas_call(
        functools.partial(_ssd_kernel, reverse=reverse, d=d, ncc=ncc, final=final, hpg=hpg, ci=ci),
        grid=(batch, ncc + nlc),
        in_specs=in_specs,
        out_specs=lat_spec(ci),
        out_shape=jax.ShapeDtypeStruct(lat[0].shape, BF16 if final else F32),
        scratch_shapes=[pltpu.VMEM((S_GROUPS, S_STATE, hpg * S_HEAD_DIM), F32)],
        compiler_params=_params(2), name="ssd_final" if final else "ssd_bwd")(*args)


def _pad_cols(a, n):
    return jnp.pad(a, ((0, 0), (0, n - a.shape[1])))


def kernel(x, c, ctx, c_ctx, w_ada, b_ada, norm_g, w_in, m_gate_b, m_norm_g, s_conv_w, s_conv_b,
           s_dt_bias, s_a_log, s_d, s_norm_g, w_bm, w_bs, w_out, w_ffn_gate, w_ffn_up, w_ffn_down):
    batch, seq, d_model = x.shape
    ctx_len = ctx.shape[1]
    depth = w_in.shape[0]
    rows = seq // GRID_W
    assert depth == 1 and rows * 2 == CHUNK and batch + 1 <= MOD_ROWS
    assert ctx_len % CHUNK == 0 and seq % CHUNK == 0
    m_qk = d_model // 2
    m_v = d_model
    s_inner = d_model
    s_heads = s_inner // S_HEAD_DIM
    hpg = s_heads // S_GROUPS
    s_bc = S_GROUPS * S_STATE
    conv_ch = s_inner + 2 * s_bc
    assert s_heads <= LANES and hpg % 2 == 0
    ncc, nlc = ctx_len // CHUNK, seq // CHUNK
    n_ctx, n_lat = batch * ctx_len, batch * seq
    li = 0

    cvec = jnp.zeros((MOD_ROWS, d_model), F32).at[:batch].set(c).at[batch].set(c_ctx)
    mod = _ada(cvec, w_ada[li], b_ada[li][None, :], tn=512)
    mod3 = mod.reshape(MOD_ROWS * 6, 1, d_model)
    g_n = norm_g[li]

    tm_tok = min(1024, n_ctx)
    u = _norm_mod(ctx.reshape(n_ctx, d_model), x.reshape(n_lat, d_model), g_n[0][None, :], mod3,
                  batch, tm=min(256, n_ctx))

    w = w_in[li]
    o0 = 0
    def take(n):
        nonlocal o0
        s = w[:, o0:o0 + n]
        o0 += n
        return s
    w_q = (take(m_qk) * (float(m_qk // M_HEADS) ** -0.5)).astype(BF16)
    w_k = take(m_qk).astype(BF16)
    w_v = take(m_v).astype(BF16)
    w_gt = take(N_DIR * 2 * M_HEADS)
    w_xbc = take(conv_ch).astype(BF16)
    w_dt = take(N_DIR * s_heads)
    w_o = take(m_v).astype(BF16)
    w_z = take(s_inner).astype(BF16)
    w_gm = take(d_model).astype(BF16)
    w_gs = take(d_model).astype(BF16)
    w_aux = jnp.concatenate([_pad_cols(w_gt, LANES)]
                            + [_pad_cols(w_dt[:, dd * s_heads:(dd + 1) * s_heads], LANES)
                               for dd in range(N_DIR)], axis=1).astype(BF16)

    tn = 512
    q = _matmul(u, w_q, BF16, tm_tok, tn, name="proj_q")
    k = _matmul(u, w_k, BF16, tm_tok, tn, name="proj_k")
    v = _matmul(u, w_v, BF16, tm_tok, tn, name="proj_v")
    aux = _matmul(u, w_aux, F32, tm_tok, LANES * (1 + N_DIR), name="proj_aux")
    xbc_c = _matmul(u, w_xbc, BF16, tm_tok, tn, 0, n_ctx, name="proj_xbc_ctx")
    xbc_l = _matmul(u, w_xbc, BF16, tm_tok, tn, n_ctx, n_lat, name="proj_xbc_lat")
    o_g = _matmul(u, w_o, BF16, tm_tok, tn, n_ctx, n_lat, name="proj_o")
    z_g = _matmul(u, w_z, BF16, tm_tok, tn, n_ctx, n_lat, name="proj_z")
    g_m = _matmul(u, w_gm, BF16, tm_tok, tn, n_ctx, n_lat, name="proj_gm")
    g_s = _matmul(u, w_gs, BF16, tm_tok, tn, n_ctx, n_lat, name="proj_gs")

    gates = aux[:, :LANES]
    gate_b = _pad_cols(m_gate_b[li].reshape(1, -1).astype(F32), LANES)
    common = dict(batch=batch, ncc=ncc, nlc=nlc)
    h_bwd = _mlstm_pass(q, k, v, gates, gate_b, d=1, reverse=True, **common)
    y_m = _mlstm_pass(q, k, v, gates, gate_b, d=0, reverse=False,
                      extra=(h_bwd, o_g, m_norm_g[li][None, :].astype(F32)), **common)

    cw, cbias = s_conv_w[li].astype(F32), s_conv_b[li][None, :].astype(F32)
    xl4 = xbc_l.reshape(batch, rows, GRID_W, conv_ch)
    xc3 = xbc_c.reshape(batch, ctx_len, conv_ch)
    cb_sz = 512
    parts = ((0, s_inner), (s_inner, s_bc), (s_inner + s_bc, s_bc))
    lat = [_conv(xl4, cw, cbias, o_, n_, cb_sz, True).reshape(batch, rows, GRID_W * n_) for o_, n_ in parts]
    ctxs = [_conv(xc3, cw, cbias, o_, n_, cb_sz, False) for o_, n_ in parts]
    dt_all = aux[:, LANES:]
    lat.append(dt_all[n_ctx:].reshape(batch, rows, GRID_W * N_DIR * LANES))
    ctxs.append(dt_all[:n_ctx].reshape(batch, ctx_len, N_DIR * LANES))
    dtb = _pad_cols(s_dt_bias[li].astype(F32), LANES)
    alog = _pad_cols(s_a_log[li].astype(F32), LANES)
    skip = jnp.repeat(s_d[li].astype(F32), S_HEAD_DIM)[None, :]
    y_bwd = _ssd_pass(lat, ctxs, dtb[1:2], alog[1:2], d=1, reverse=True, hpg=hpg, **common)
    z3 = z_g.reshape(batch, rows, GRID_W * s_inner)
    y_s = _ssd_pass(lat, ctxs, dtb[0:1], alog[0:1], d=0, reverse=False, hpg=hpg,
                    extra=(y_bwd, z3, skip, s_norm_g[li][None, :].astype(F32)), **common)
    y_s = y_s.reshape(n_lat, s_inner)

    tm_lat = min(1024, n_lat)
    mix_pre = _merge(y_m, y_s, w_bm[li].astype(BF16), w_bs[li].astype(BF16), g_m, g_s, tm_lat, 256)
    mix = _matmul(mix_pre, w_out[li].astype(BF16), F32, tm_lat, tn, name="out_proj")
    x2 = x.reshape(n_lat, d_model)
    x_new, h_mod = _resid_mid(mix, x2, g_n[1][None, :], g_n[2][None, :], mod3, batch, tm=128)

    act = _swiglu_up(h_mod, w_ffn_gate[li].astype(BF16), w_ffn_up[li].astype(BF16), tm_lat, 256)
    hf = _matmul(act, w_ffn_down[li].astype(BF16), F32, min(512, n_lat), 256, name="ffn_down")
    out = _resid_out(hf, x_new, g_n[3][None, :], mod3, batch, tm=128)
    return out.reshape(batch, seq, d_model)
```

```python
import functools

import jax
import jax.numpy as jnp
from jax import lax
from jax.experimental import pallas as pl
from jax.experimental.pallas import tpu as pltpu

F32 = jnp.float32
BF16 = jnp.bfloat16

CHUNK = 128
GRID_W = 64
EPS = 1e-6
N_DIR = 2
M_HEADS = 8
S_GROUPS = 8
S_HEAD_DIM = 64
S_STATE = 128
S_CONV = 4
LANES = 128
BF16_ROWS = 16
MOD_ROWS = 8
W_ALIGN = 32
VMEM_LIMIT = 56 * 1024 * 1024


def _params(n_axes):
    return pltpu.CompilerParams(dimension_semantics=("arbitrary",) * n_axes,
                                vmem_limit_bytes=VMEM_LIMIT)


def _dot(a, b):
    return jnp.dot(a, b, preferred_element_type=F32)


def _dot_nt(a, b):
    return lax.dot_general(a, b, (((1,), (1,)), ((), ())), preferred_element_type=F32)


def _dot_tn(a, b):
    return lax.dot_general(a, b, (((0,), (0,)), ((), ())), preferred_element_type=F32)


def _cumsum_time(tri, x):
    hi = x.astype(BF16)
    r1 = x - hi.astype(F32)
    mid = r1.astype(BF16)
    lo = (r1 - mid.astype(F32)).astype(BF16)
    return _dot(tri, hi) + _dot(tri, mid) + _dot(tri, lo)


def _sigmoid(x):
    return 1.0 / (1.0 + jnp.exp(-x))


def _silu(x):
    return x * _sigmoid(x)


def _softplus(x):
    return jnp.maximum(x, 0.0) + jnp.log(1.0 + jnp.exp(-jnp.abs(x)))


def _time_masks(reverse):
    jj = lax.broadcasted_iota(jnp.int32, (CHUNK, CHUNK), 0)
    ss = lax.broadcasted_iota(jnp.int32, (CHUNK, CHUNK), 1)
    mask = (ss >= jj) if reverse else (ss <= jj)
    return mask, jnp.where(mask, 1.0, 0.0).astype(BF16)


def _mm_kernel(a_ref, w_ref, o_ref):
    o_ref[...] = _dot(a_ref[...].astype(BF16), w_ref[...].astype(BF16)).astype(o_ref.dtype)


def _matmul(a, w, out_dtype, tm, tn, row_off=0, rows=None, col_off=0, cols=None, name="matmul"):
    k = a.shape[1]
    rows = a.shape[0] - row_off if rows is None else rows
    cols = w.shape[1] - col_off if cols is None else cols
    assert rows % tm == 0 and cols % tn == 0 and row_off % tm == 0 and col_off % tn == 0
    roff, coff = row_off // tm, col_off // tn
    return pl.pallas_call(
        _mm_kernel,
        grid=(rows // tm, cols // tn),
        in_specs=[pl.BlockSpec((tm, k), lambda i, j: (i + roff, 0)),
                  pl.BlockSpec((k, tn), lambda i, j: (0, j + coff))],
        out_specs=pl.BlockSpec((tm, tn), lambda i, j: (i, j)),
        out_shape=jax.ShapeDtypeStruct((rows, cols), out_dtype),
        compiler_params=_params(2), name=name)(a, w)


def _wprep_kernel(off_ref, w_ref, o_ref, *, q_blocks, q_scale):
    del off_ref
    scale = jnp.where(pl.program_id(0) < q_blocks, q_scale, 1.0)
    o_ref[...] = (w_ref[...] * scale).T.astype(BF16)


def _wprep(wt, offsets, tn, q_blocks, q_scale):
    k = wt.shape[1]
    nblk = len(offsets)
    assert all(o % W_ALIGN == 0 for o in offsets)
    return pl.pallas_call(
        functools.partial(_wprep_kernel, q_blocks=q_blocks, q_scale=q_scale),
        grid_spec=pltpu.PrefetchScalarGridSpec(
            num_scalar_prefetch=1, grid=(nblk,),
            in_specs=[pl.BlockSpec((pl.Element(tn), pl.Element(k)),
                                   lambda j, off: (pl.multiple_of(off[j], W_ALIGN), 0))],
            out_specs=pl.BlockSpec((k, tn), lambda j, off: (0, j))),
        out_shape=jax.ShapeDtypeStruct((k, nblk * tn), BF16),
        compiler_params=_params(1), name="w_in_prep")(jnp.asarray(offsets, jnp.int32), wt)


def _ada_kernel(c_ref, w_ref, b_ref, o_ref):
    a = _silu(c_ref[...]).astype(BF16)
    o_ref[...] = _dot(a, w_ref[...].astype(BF16)) + b_ref[...]


def _ada(cvec, w, b, tn):
    k, n = w.shape
    return pl.pallas_call(
        _ada_kernel,
        grid=(n // tn,),
        in_specs=[pl.BlockSpec((MOD_ROWS, k), lambda j: (0, 0)),
                  pl.BlockSpec((k, tn), lambda j: (0, j)),
                  pl.BlockSpec((1, tn), lambda j: (0, j))],
        out_specs=pl.BlockSpec((MOD_ROWS, tn), lambda j: (0, j)),
        out_shape=jax.ShapeDtypeStruct((MOD_ROWS, n), F32),
        compiler_params=_params(1), name="ada")(cvec, w, b)


def _merge_kernel(ym_ref, ys_ref, wm_ref, ws_ref, gm_ref, gs_ref, o_ref):
    am = _dot(ym_ref[...], wm_ref[...])
    a_s = _dot(ys_ref[...], ws_ref[...])
    o_ref[...] = (_sigmoid(gm_ref[...].astype(F32)) * am
                  + _sigmoid(gs_ref[...].astype(F32)) * a_s).astype(o_ref.dtype)


def _merge(ym, ys, wm, ws, gates, gm_col, gs_col, tm, tn):
    m, k = ym.shape
    n = wm.shape[1]
    assert gm_col % tn == 0 and gs_col % tn == 0
    a_spec = pl.BlockSpec((tm, k), lambda i, j: (i, 0))
    w_spec = pl.BlockSpec((k, tn), lambda i, j: (0, j))
    return pl.pallas_call(
        _merge_kernel,
        grid=(m // tm, n // tn),
        in_specs=[a_spec, a_spec, w_spec, w_spec,
                  pl.BlockSpec((tm, tn), lambda i, j: (i, j + gm_col // tn)),
                  pl.BlockSpec((tm, tn), lambda i, j: (i, j + gs_col // tn))],
        out_specs=pl.BlockSpec((tm, tn), lambda i, j: (i, j)),
        out_shape=jax.ShapeDtypeStruct((m, n), BF16),
        compiler_params=_params(2), name="merge")(ym, ys, wm, ws, gates, gates)


def _swiglu_kernel(a_ref, wg_ref, wu_ref, o_ref):
    a = a_ref[...]
    o_ref[...] = (_silu(_dot(a, wg_ref[...])) * _dot(a, wu_ref[...])).astype(o_ref.dtype)


def _swiglu_up(a, wg, wu, tm, tn):
    m, k = a.shape
    n = wg.shape[1]
    w_spec = pl.BlockSpec((k, tn), lambda i, j: (0, j))
    return pl.pallas_call(
        _swiglu_kernel,
        grid=(m // tm, n // tn),
        in_specs=[pl.BlockSpec((tm, k), lambda i, j: (i, 0)), w_spec, w_spec],
        out_specs=pl.BlockSpec((tm, tn), lambda i, j: (i, j)),
        out_shape=jax.ShapeDtypeStruct((m, n), BF16),
        compiler_params=_params(2), name="swiglu_up")(a, wg, wu)


def _rms(x, g):
    return x * lax.rsqrt(jnp.mean(x * x, axis=-1, keepdims=True) + EPS) * g


def _norm_mod_kernel(xl_ref, xc_ref, g_ref, sh_ref, sc_ref, o_ref, *, n_lat_blocks):
    i = pl.program_id(0)

    def emit(x):
        o_ref[...] = (_rms(x, g_ref[...]) * (1.0 + sc_ref[0]) + sh_ref[0]).astype(o_ref.dtype)

    @pl.when(i < n_lat_blocks)
    def _():
        emit(xl_ref[...])

    @pl.when(i >= n_lat_blocks)
    def _():
        emit(xc_ref[...])


def _norm_mod(xl, xc, g, mod3, batch, tm):
    d = xl.shape[1]
    nlb = xl.shape[0] // tm
    ncb = xc.shape[0] // tm
    per_batch = nlb // batch

    def mod_row(i):
        return jnp.where(i < nlb, jnp.minimum(i, nlb - 1) // per_batch, batch)

    return pl.pallas_call(
        functools.partial(_norm_mod_kernel, n_lat_blocks=nlb),
        grid=(nlb + ncb,),
        in_specs=[pl.BlockSpec((tm, d), lambda i: (jnp.minimum(i, nlb - 1), 0)),
                  pl.BlockSpec((tm, d), lambda i: (jnp.maximum(i - nlb, 0), 0)),
                  pl.BlockSpec((1, d), lambda i: (0, 0)),
                  pl.BlockSpec((1, 1, d), lambda i: (mod_row(i) * 6 + 0, 0, 0)),
                  pl.BlockSpec((1, 1, d), lambda i: (mod_row(i) * 6 + 1, 0, 0))],
        out_specs=pl.BlockSpec((tm, d), lambda i: (i, 0)),
        out_shape=jax.ShapeDtypeStruct(((nlb + ncb) * tm, d), BF16),
        compiler_params=_params(1), name="norm_mod")(xl, xc, g, mod3, mod3)


def _resid_mid_kernel(mix_ref, x_ref, g1_ref, g2_ref, ga_ref, sh_ref, sc_ref, xn_ref, h_ref):
    xn = x_ref[...] + ga_ref[0] * _rms(mix_ref[...].astype(F32), g1_ref[...])
    xn_ref[...] = xn
    h_ref[...] = (_rms(xn, g2_ref[...]) * (1.0 + sc_ref[0]) + sh_ref[0]).astype(h_ref.dtype)


def _resid_mid(mix, x, g1, g2, mod3, batch, tm):
    m, d = x.shape
    per_batch = m // tm // batch
    row = pl.BlockSpec((tm, d), lambda i: (i, 0))
    vec = pl.BlockSpec((1, d), lambda i: (0, 0))

    def mod(which):
        return pl.BlockSpec((1, 1, d), lambda i: ((i // per_batch) * 6 + which, 0, 0))

    return pl.pallas_call(
        _resid_mid_kernel,
        grid=(m // tm,),
        in_specs=[row, row, vec, vec, mod(2), mod(3), mod(4)],
        out_specs=[row, row],
        out_shape=[jax.ShapeDtypeStruct((m, d), F32), jax.ShapeDtypeStruct((m, d), BF16)],
        compiler_params=_params(1), name="resid_mid")(mix, x, g1, g2, mod3, mod3, mod3)


def _resid_out_kernel(hf_ref, x_ref, g_ref, ga_ref, o_ref):
    o_ref[...] = x_ref[...] + ga_ref[0] * _rms(hf_ref[...].astype(F32), g_ref[...])


def _resid_out(hf, x, g, mod3, batch, tm):
    m, d = x.shape
    per_batch = m // tm // batch
    row = pl.BlockSpec((tm, d), lambda i: (i, 0))
    return pl.pallas_call(
        _resid_out_kernel,
        grid=(m // tm,),
        in_specs=[row, row, pl.BlockSpec((1, d), lambda i: (0, 0)),
                  pl.BlockSpec((1, 1, d), lambda i: ((i // per_batch) * 6 + 5, 0, 0))],
        out_specs=row,
        out_shape=jax.ShapeDtypeStruct((m, d), F32),
        compiler_params=_params(1), name="resid_out")(hf, x, g, mod3)


def _mlstm_kernel(*refs, reverse, d, final, dk, dv):
    if final:
        (q_ref, k_ref, v_ref, g_ref, gb_ref, hb_ref, og_ref, hg_ref,
         o_ref, ct_sc, n_sc, m_sc) = refs
    else:
        q_ref, k_ref, v_ref, g_ref, gb_ref, o_ref, ct_sc, n_sc, m_sc = refs

    @pl.when(pl.program_id(1) == 0)
    def _():
        ct_sc[...] = jnp.zeros_like(ct_sc)
        n_sc[...] = jnp.zeros_like(n_sc)
        m_sc[...] = jnp.zeros_like(m_sc)

    mask, tri = _time_masks(reverse)
    last = 0 if reverse else CHUNK - 1

    g = g_ref[...] + gb_ref[...]
    log_f = jnp.minimum(g, 0.0) - jnp.log(1.0 + jnp.exp(-jnp.abs(g)))
    csum = _cumsum_time(tri, log_f)
    off_i = d * 2 * M_HEADS
    off_f = off_i + M_HEADS
    li_t = g if off_i == 0 else pltpu.roll(g, LANES - off_i, axis=1)
    b_t = pltpu.roll(csum, LANES - off_f, axis=1)
    r_t = (li_t - b_t).T

    for h in range(M_HEADS):
        b_c = b_t[:, h:h + 1]
        li_c = li_t[:, h:h + 1]
        logd = jnp.where(mask, b_c + r_t[h:h + 1, :], -jnp.inf)
        m_prev = m_sc[h:h + 1, 0:1]
        m_carry = b_c + m_prev
        m_row = jnp.maximum(m_carry, jnp.max(logd, axis=1, keepdims=True))
        d_mat = jnp.exp(logd - m_row)
        w_carry = jnp.exp(m_carry - m_row)

        q = q_ref[:, h * dk:(h + 1) * dk]
        k = k_ref[:, h * dk:(h + 1) * dk]
        v = v_ref[:, h * dv:(h + 1) * dv]
        ct = ct_sc[h]
        n_row = n_sc[h:h + 1, :]

        s = _dot_nt(q, k) * d_mat
        num = _dot(s.astype(BF16), v) + w_carry * _dot(q, ct.astype(BF16))
        den = (jnp.sum(s, axis=1, keepdims=True)
               + w_carry * jnp.sum(q.astype(F32) * n_row, axis=1, keepdims=True))
        h_out = num / jnp.maximum(jnp.abs(den), jnp.exp(-m_row))

        m_end = m_row[last:last + 1, :]
        b_end = b_c[last:last + 1, :]
        w_state = jnp.exp(b_end - b_c + li_c - m_end)
        decay = jnp.exp(b_end + m_prev - m_end)
        ct_sc[h] = decay * ct + _dot_tn(k, (w_state * v.astype(F32)).astype(BF16))
        n_sc[h:h + 1, :] = decay * n_row + jnp.sum(w_state * k.astype(F32), axis=0, keepdims=True)
        m_sc[h:h + 1, :] = jnp.broadcast_to(m_end, (1, LANES))

        cols = slice(h * dv, (h + 1) * dv)
        if final:
            hn = _rms(h_out + hb_ref[:, cols].astype(F32), hg_ref[:, cols])
            o_ref[:, cols] = (hn * _sigmoid(og_ref[:, cols].astype(F32))).astype(o_ref.dtype)
        else:
            o_ref[:, cols] = h_out.astype(o_ref.dtype)


def _mlstm_pass(qkv, gates, gate_b, *, batch, ncc, nlc, d, reverse, extra=None):
    hv = qkv.shape[1] // 2
    hk = hv // 2
    dk, dv = hk // M_HEADS, hv // M_HEADS
    final = extra is not None

    def comb_idx(b, t):
        cc = (ncc - 1 - t) if reverse else t
        lc = (nlc - 1 - (t - ncc)) if reverse else (t - ncc)
        return jnp.where(t < ncc, batch * nlc + b * ncc + cc, b * nlc + lc)

    def lat_idx(b, t):
        tt = jnp.maximum(t - ncc, 0)
        return b * nlc + ((nlc - 1 - tt) if reverse else tt)

    lat_spec = pl.BlockSpec((CHUNK, hv), lambda b, t: (lat_idx(b, t), 0))
    in_specs = [pl.BlockSpec((CHUNK, hk), lambda b, t: (comb_idx(b, t), 0)),
                pl.BlockSpec((CHUNK, hk), lambda b, t: (comb_idx(b, t), 1)),
                pl.BlockSpec((CHUNK, hv), lambda b, t: (comb_idx(b, t), 1)),
                pl.BlockSpec((CHUNK, LANES), lambda b, t: (comb_idx(b, t), 0)),
                pl.BlockSpec((1, LANES), lambda b, t: (0, 0))]
    args = [qkv, qkv, qkv, gates, gate_b]
    if final:
        in_specs += [lat_spec, lat_spec, pl.BlockSpec((1, hv), lambda b, t: (0, 0))]
        args += list(extra)
    return pl.pallas_call(
        functools.partial(_mlstm_kernel, reverse=reverse, d=d, final=final, dk=dk, dv=dv),
        grid=(batch, ncc + nlc),
        in_specs=in_specs,
        out_specs=lat_spec,
        out_shape=jax.ShapeDtypeStruct((batch * nlc * CHUNK, hv), BF16),
        scratch_shapes=[pltpu.VMEM((M_HEADS, dk, dv), F32),
                        pltpu.VMEM((M_HEADS, dk), F32),
                        pltpu.VMEM((M_HEADS, LANES), F32)],
        compiler_params=_params(2), name="mlstm_final" if final else "mlstm_bwd")(*args)


def _store_col_major(o_ref, scr):
    rows, cols, _ = scr.shape
    step = BF16_ROWS
    for p in range(cols // step):
        piece = pltpu.einshape("rcd->crd", scr[:, p * step:(p + 1) * step, :])
        o_ref[0, p * step * rows:(p + 1) * step * rows, :] = piece.reshape(step * rows, piece.shape[-1])


def _conv_lat_kernel(x_ref, w_ref, b_ref, o_ref, scr):
    rows, cols = x_ref.shape[1], x_ref.shape[2]
    w = w_ref[...]
    w0, w1, w2, w3, bias = w[0:1, :], w[1:2, :], w[2:3, :], w[3:4, :], b_ref[...]
    cidx = lax.broadcasted_iota(jnp.int32, (cols, 1), 0)

    def ld(r):
        return x_ref[0, r].astype(F32)

    def from_prev_col(a):
        return jnp.where(cidx >= 1, pltpu.roll(a, 1, axis=0), 0.0)

    def from_next_col(a):
        return jnp.where(cidx <= cols - 2, pltpu.roll(a, cols - 1, axis=0), 0.0)

    def emit(r, prev, cur, nxt1, nxt2):
        scr[r] = _silu(w0 * prev + w1 * cur + w2 * nxt1 + w3 * nxt2 + bias).astype(scr.dtype)

    emit(0, from_prev_col(ld(rows - 1)), ld(0), ld(1), ld(2))

    def body(r, carry):
        emit(r, ld(r - 1), ld(r), ld(r + 1), ld(r + 2))
        return carry

    lax.fori_loop(1, rows - 2, body, 0)
    top0 = from_next_col(ld(0))
    emit(rows - 2, ld(rows - 3), ld(rows - 2), ld(rows - 1), top0)
    emit(rows - 1, ld(rows - 2), ld(rows - 1), top0, from_next_col(ld(1)))
    _store_col_major(o_ref, scr)


def _conv_lat(x4, w, b, ch_off, ch_n, cb, t_all):
    assert ch_off % cb == 0 and ch_n % cb == 0
    off = ch_off // cb
    bsz, rows, cols, _ = x4.shape
    return pl.pallas_call(
        _conv_lat_kernel,
        grid=(bsz, ch_n // cb),
        in_specs=[pl.BlockSpec((1, rows, cols, cb), lambda b_, j: (b_, 0, 0, j + off)),
                  pl.BlockSpec((S_CONV, cb), lambda b_, j: (0, j + off)),
                  pl.BlockSpec((1, cb), lambda b_, j: (0, j + off))],
        out_specs=pl.BlockSpec((1, rows * cols, cb), lambda b_, j: (b_, 0, j)),
        out_shape=jax.ShapeDtypeStruct((bsz, t_all, ch_n), BF16),
        scratch_shapes=[pltpu.VMEM((rows, cols, cb), BF16)],
        compiler_params=_params(2), name="conv_lat")(x4, w, b)


def _conv_ctx_kernel(x_ref, w_ref, b_ref, dst_ref, o_ref):
    del dst_ref
    tlen = x_ref.shape[1]
    w = w_ref[...]
    x = x_ref[0].astype(F32)
    tidx = lax.broadcasted_iota(jnp.int32, (tlen, 1), 0)
    prev = jnp.where(tidx >= 1, pltpu.roll(x, 1, axis=0), 0.0)
    nxt1 = jnp.where(tidx <= tlen - 2, pltpu.roll(x, tlen - 1, axis=0), 0.0)
    nxt2 = jnp.where(tidx <= tlen - 3, pltpu.roll(x, tlen - 2, axis=0), 0.0)
    o_ref[0] = _silu(w[0:1, :] * prev + w[1:2, :] * x + w[2:3, :] * nxt1 + w[3:4, :] * nxt2
                     + b_ref[...]).astype(o_ref.dtype)


def _conv_ctx(x3, w, b, dst, ch_off, cb, t_lat):
    bsz, tlen, _ = x3.shape
    ch_n = dst.shape[2]
    assert ch_off % cb == 0 and ch_n % cb == 0 and t_lat % tlen == 0
    off = ch_off // cb
    return pl.pallas_call(
        _conv_ctx_kernel,
        grid=(bsz, ch_n // cb),
        in_specs=[pl.BlockSpec((1, tlen, cb), lambda b_, j: (b_, 0, j + off)),
                  pl.BlockSpec((S_CONV, cb), lambda b_, j: (0, j + off)),
                  pl.BlockSpec((1, cb), lambda b_, j: (0, j + off)),
                  pl.BlockSpec(memory_space=pl.ANY)],
        out_specs=pl.BlockSpec((1, tlen, cb), lambda b_, j: (b_, t_lat // tlen, j)),
        out_shape=jax.ShapeDtypeStruct(dst.shape, dst.dtype),
        input_output_aliases={3: 0},
        compiler_params=_params(2), name="conv_ctx")(x3, w, b, dst)


def _dt_lat_kernel(x_ref, o_ref):
    _store_col_major(o_ref, x_ref.at[0])


def _dt_ctx_kernel(x_ref, dst_ref, o_ref):
    del dst_ref
    o_ref[...] = x_ref[...]


def _dt_layout(dt_lat4, dt_ctx3, t_all):
    bsz, rows, cols, n = dt_lat4.shape
    tlen = dt_ctx3.shape[1]
    t_lat = rows * cols
    out = pl.pallas_call(
        _dt_lat_kernel,
        grid=(bsz,),
        in_specs=[pl.BlockSpec((1, rows, cols, n), lambda b_: (b_, 0, 0, 0))],
        out_specs=pl.BlockSpec((1, t_lat, n), lambda b_: (b_, 0, 0)),
        out_shape=jax.ShapeDtypeStruct((bsz, t_all, n), F32),
        compiler_params=_params(1), name="dt_lat")(dt_lat4)
    return pl.pallas_call(
        _dt_ctx_kernel,
        grid=(bsz,),
        in_specs=[pl.BlockSpec((1, tlen, n), lambda b_: (b_, 0, 0)),
                  pl.BlockSpec(memory_space=pl.ANY)],
        out_specs=pl.BlockSpec((1, tlen, n), lambda b_: (b_, t_lat // tlen, 0)),
        out_shape=jax.ShapeDtypeStruct(out.shape, out.dtype),
        input_output_aliases={1: 0},
        compiler_params=_params(1), name="dt_ctx")(dt_ctx3, out)


def _expand_heads(tile, first, hpg, lane_lo):
    m = tile.shape[0]
    cols = []
    for p in range(hpg // 2):
        a = jnp.broadcast_to(tile[:, first + 2 * p:first + 2 * p + 1], (m, LANES))
        b = jnp.broadcast_to(tile[:, first + 2 * p + 1:first + 2 * p + 2], (m, LANES))
        cols.append(jnp.where(lane_lo[:m], a, b))
    return cols[0] if len(cols) == 1 else jnp.concatenate(cols, axis=1)


def _ssd_kernel(*refs, reverse, col0, final, hpg):
    if final:
        (x_ref, b_ref, c_ref, dt_ref, dtb_ref, alog_ref, yb_ref, skip_ref, o_ref, h_sc) = refs
    else:
        (x_ref, b_ref, c_ref, dt_ref, dtb_ref, alog_ref, o_ref, h_sc) = refs
    gw = hpg * S_HEAD_DIM

    @pl.when(pl.program_id(1) == 0)
    def _():
        h_sc[...] = jnp.zeros_like(h_sc)

    mask, tri = _time_masks(reverse)
    last = 0 if reverse else CHUNK - 1
    lane_lo = lax.broadcasted_iota(jnp.int32, (CHUNK, LANES), 1) < S_HEAD_DIM
    lane_head = lax.broadcasted_iota(jnp.int32, (CHUNK, gw), 1) // S_HEAD_DIM

    dt = _softplus(dt_ref[0] + dtb_ref[...])
    cum = _cumsum_time(tri, -dt * jnp.exp(alog_ref[...]))
    cum_t = cum.T
    cum_end = cum[last:last + 1, :]
    e_cum = jnp.exp(cum)
    e_rest = jnp.exp(cum_end - cum)
    e_end = jnp.exp(cum_end)

    for g in range(S_GROUPS):
        first = col0 + g * hpg
        cols = slice(g * gw, (g + 1) * gw)
        xg = x_ref[0, :, cols].astype(F32)
        bg = b_ref[0, :, g * S_STATE:(g + 1) * S_STATE]
        cg = c_ref[0, :, g * S_STATE:(g + 1) * S_STATE]
        cb = _dot_nt(cg, bg)
        ws = []
        for r in range(hpg):
            idx = first + r
            seg = jnp.where(mask, cum[:, idx:idx + 1] - cum_t[idx:idx + 1, :], -jnp.inf)
            ws.append((cb * jnp.exp(seg)).astype(BF16))
        w_cat = jnp.concatenate(ws, axis=1)
        xdt = xg * _expand_heads(dt, first, hpg, lane_lo)
        xdt_b = xdt.astype(BF16)
        bd = jnp.concatenate([jnp.where(lane_head == r, xdt_b, jnp.zeros_like(xdt_b))
                              for r in range(hpg)], axis=0)
        h_t = h_sc[g]
        y = (_dot(w_cat, bd)
             + _expand_heads(e_cum, first, hpg, lane_lo) * _dot(cg, h_t.astype(BF16)))
        wx = (xdt * _expand_heads(e_rest, first, hpg, lane_lo)).astype(BF16)
        h_sc[g] = _expand_heads(e_end, first, hpg, lane_lo) * h_t + _dot_tn(bg, wx)
        if final:
            y = y + yb_ref[0, :, cols].astype(F32) + skip_ref[:, cols] * xg
        o_ref[0, :, cols] = y.astype(o_ref.dtype)


def _ssd_pass(xs, bm, cm, dt, dt_bias, a_log, *, ncc, nlc, col0, reverse, hpg, extra=None):
    bsz, _, ci = xs.shape
    sbc = S_GROUPS * S_STATE
    final = extra is not None

    def any_idx(b, t):
        cc = (ncc - 1 - t) if reverse else t
        lc = (nlc - 1 - (t - ncc)) if reverse else (t - ncc)
        return (b, jnp.where(t < ncc, nlc + cc, lc), 0)

    def lat_idx(b, t):
        tt = jnp.maximum(t - ncc, 0)
        return (b, (nlc - 1 - tt) if reverse else tt, 0)

    vec = pl.BlockSpec((1, LANES), lambda b, t: (0, 0))
    in_specs = [pl.BlockSpec((1, CHUNK, ci), any_idx), pl.BlockSpec((1, CHUNK, sbc), any_idx),
                pl.BlockSpec((1, CHUNK, sbc), any_idx), pl.BlockSpec((1, CHUNK, LANES), any_idx),
                vec, vec]
    args = [xs, bm, cm, dt, dt_bias, a_log]
    if final:
        in_specs += [pl.BlockSpec((1, CHUNK, ci), lat_idx), pl.BlockSpec((1, ci), lambda b, t: (0, 0))]
        args += list(extra)
    return pl.pallas_call(
        functools.partial(_ssd_kernel, reverse=reverse, col0=col0, final=final, hpg=hpg),
        grid=(bsz, ncc + nlc),
        in_specs=in_specs,
        out_specs=pl.BlockSpec((1, CHUNK, ci), lat_idx),
        out_shape=jax.ShapeDtypeStruct((bsz, nlc * CHUNK, ci), BF16),
        scratch_shapes=[pltpu.VMEM((S_GROUPS, S_STATE, hpg * S_HEAD_DIM), F32)],
        compiler_params=_params(2), name="ssd_final" if final else "ssd_bwd")(*args)


def _ssd_out_kernel(y_ref, z_ref, g_ref, o_ref):
    y = pltpu.einshape("crd->rcd", y_ref[0]).astype(F32)
    yt = y * _silu(z_ref[0].astype(F32))
    o_ref[0] = _rms(yt, g_ref[...]).astype(o_ref.dtype)


def _ssd_out(y_cm, z4, z_col, norm_g, gw):
    bsz, cols, rows, ci = y_cm.shape
    rb = BF16_ROWS
    assert z_col % gw == 0 and rows % rb == 0
    zoff = z_col // gw
    return pl.pallas_call(
        _ssd_out_kernel,
        grid=(bsz, rows // rb, ci // gw),
        in_specs=[pl.BlockSpec((1, cols, rb, gw), lambda b, r, g: (b, 0, r, g)),
                  pl.BlockSpec((1, rb, cols, gw), lambda b, r, g: (b, r, 0, g + zoff)),
                  pl.BlockSpec((1, gw), lambda b, r, g: (0, g))],
        out_specs=pl.BlockSpec((1, rb, cols, gw), lambda b, r, g: (b, r, 0, g)),
        out_shape=jax.ShapeDtypeStruct((bsz, rows, cols, ci), BF16),
        compiler_params=_params(3), name="ssd_out")(y_cm, z4, norm_g)


def _pad_cols(a, n):
    return jnp.pad(a, ((0, 0), (0, n - a.shape[1])))


def kernel(x, c, ctx, c_ctx, w_ada, b_ada, norm_g, w_in, m_gate_b, m_norm_g, s_conv_w, s_conv_b,
           s_dt_bias, s_a_log, s_d, s_norm_g, w_bm, w_bs, w_out, w_ffn_gate, w_ffn_up, w_ffn_down):
    batch, seq, d_model = x.shape
    ctx_len = ctx.shape[1]
    depth = w_in.shape[0]
    rows = seq // GRID_W
    assert depth == 1 and rows * 2 == CHUNK and batch + 1 <= MOD_ROWS
    assert ctx_len % CHUNK == 0 and seq % CHUNK == 0
    m_qk = d_model // 2
    m_v = d_model
    s_inner = d_model
    s_heads = s_inner // S_HEAD_DIM
    hpg = s_heads // S_GROUPS
    gw = hpg * S_HEAD_DIM
    s_bc = S_GROUPS * S_STATE
    conv_ch = s_inner + 2 * s_bc
    n_gates = N_DIR * 2 * M_HEADS
    assert N_DIR * s_heads <= LANES and n_gates <= LANES and hpg % 2 == 0
    ncc, nlc = ctx_len // CHUNK, seq // CHUNK
    n_ctx, n_lat = batch * ctx_len, batch * seq
    t_all = seq + ctx_len
    li = 0

    cvec = jnp.zeros((MOD_ROWS, d_model), F32).at[:batch].set(c).at[batch].set(c_ctx)
    mod = _ada(cvec, w_ada[li], b_ada[li][None, :], tn=512)
    mod3 = mod.reshape(MOD_ROWS * 6, 1, d_model)
    g_n = norm_g[li]

    u = _norm_mod(x.reshape(n_lat, d_model), ctx.reshape(n_ctx, d_model), g_n[0][None, :], mod3,
                  batch, tm=min(256, n_ctx))

    wt = w_in[li].T
    tn = 512
    src, o0 = {}, 0
    for name, n in (("q", m_qk), ("k", m_qk), ("v", m_v), ("gates", n_gates), ("xbc", conv_ch),
                    ("dt", N_DIR * s_heads), ("o", m_v), ("z", s_inner), ("gm", d_model), ("gs", d_model)):
        src[name] = (o0, n)
        o0 += n
    packed = ("q", "k", "v", "xbc", "o", "z", "gm", "gs")
    offsets, dst, p0 = [], {}, 0
    for name in packed:
        s0, n = src[name]
        assert n % tn == 0
        offsets += [s0 + i * tn for i in range(n // tn)]
        dst[name] = p0
        p0 += n
    w_pk = _wprep(wt, offsets, tn, q_blocks=m_qk // tn, q_scale=float(m_qk // M_HEADS) ** -0.5)
    w_aux = _wprep(wt, [src["gates"][0], src["dt"][0]], LANES, q_blocks=0, q_scale=1.0)

    tm_tok = min(1024, n_ctx)
    qkv = _matmul(u, w_pk, BF16, tm_tok, tn, 0, None, dst["q"], 2 * m_qk + m_v, name="proj_qkv")
    xbc_l = _matmul(u, w_pk, BF16, tm_tok, tn, 0, n_lat, dst["xbc"], conv_ch, name="proj_xbc_lat")
    xbc_c = _matmul(u, w_pk, BF16, tm_tok, tn, n_lat, n_ctx, dst["xbc"], conv_ch, name="proj_xbc_ctx")
    ozg = _matmul(u, w_pk, BF16, tm_tok, tn, 0, n_lat, dst["o"], m_v + s_inner + 2 * d_model,
                  name="proj_ozg")
    gates = _matmul(u, w_aux, F32, tm_tok, LANES, 0, None, 0, LANES, name="proj_gates")
    dt_l = _matmul(u, w_aux, F32, tm_tok, LANES, 0, n_lat, LANES, LANES, name="proj_dt_lat")
    dt_c = _matmul(u, w_aux, F32, tm_tok, LANES, n_lat, n_ctx, LANES, LANES, name="proj_dt_ctx")

    gate_b = _pad_cols(m_gate_b[li].reshape(1, -1).astype(F32), LANES)
    common = dict(batch=batch, ncc=ncc, nlc=nlc)
    h_bwd = _mlstm_pass(qkv, gates, gate_b, d=1, reverse=True, **common)
    y_m = _mlstm_pass(qkv, gates, gate_b, d=0, reverse=False,
                      extra=(h_bwd, ozg, m_norm_g[li][None, :].astype(F32)), **common)

    cw, cbias = s_conv_w[li].astype(F32), s_conv_b[li][None, :].astype(F32)
    xl4 = xbc_l.reshape(batch, rows, GRID_W, conv_ch)
    xc3 = xbc_c.reshape(batch, ctx_len, conv_ch)
    cb_sz = 512
    scan_in = []
    for o_, n_ in ((0, s_inner), (s_inner, s_bc), (s_inner + s_bc, s_bc)):
        part = _conv_lat(xl4, cw, cbias, o_, n_, cb_sz, t_all)
        scan_in.append(_conv_ctx(xc3, cw, cbias, part, o_, cb_sz, seq))
    scan_in.append(_dt_layout(dt_l.reshape(batch, rows, GRID_W, LANES),
                              dt_c.reshape(batch, ctx_len, LANES), t_all))
    dtb = _pad_cols(s_dt_bias[li].reshape(1, -1).astype(F32), LANES)
    alog = _pad_cols(s_a_log[li].reshape(1, -1).astype(F32), LANES)
    skip = jnp.repeat(s_d[li].astype(F32), S_HEAD_DIM)[None, :]
    scan = dict(ncc=ncc, nlc=nlc, hpg=hpg)
    y_bwd = _ssd_pass(*scan_in, dtb, alog, col0=s_heads, reverse=True, **scan)
    y_tot = _ssd_pass(*scan_in, dtb, alog, col0=0, reverse=False, extra=(y_bwd, skip), **scan)
    y_s = _ssd_out(y_tot.reshape(batch, GRID_W, rows, s_inner),
                   ozg.reshape(batch, rows, GRID_W, ozg.shape[1]), dst["z"] - dst["o"],
                   s_norm_g[li][None, :].astype(F32), gw).reshape(n_lat, s_inner)

    tm_lat = min(1024, n_lat)
    mix_pre = _merge(y_m, y_s, w_bm[li].astype(BF16), w_bs[li].astype(BF16), ozg,
                     dst["gm"] - dst["o"], dst["gs"] - dst["o"], tm_lat, 256)
    mix = _matmul(mix_pre, w_out[li].astype(BF16), F32, tm_lat, tn, name="out_proj")
    x2 = x.reshape(n_lat, d_model)
    x_new, h_mod = _resid_mid(mix, x2, g_n[1][None, :], g_n[2][None, :], mod3, batch, tm=128)

    act = _swiglu_up(h_mod, w_ffn_gate[li].astype(BF16), w_ffn_up[li].astype(BF16), tm_lat, 256)
    hf = _matmul(act, w_ffn_down[li].astype(BF16), F32, min(512, n_lat), 256, name="ffn_down")
    out = _resid_out(hf, x_new, g_n[3][None, :], mod3, batch, tm=128)
    return out.reshape(batch, seq, d_model)
```

```python
import functools

import jax
import jax.numpy as jnp
from jax import lax
from jax.experimental import pallas as pl
from jax.experimental.pallas import tpu as pltpu

F32 = jnp.float32
BF16 = jnp.bfloat16

CHUNK = 128
GRID_W = 64
EPS = 1e-6
N_DIR = 2
M_HEADS = 8
S_GROUPS = 8
S_HEAD_DIM = 64
S_STATE = 128
S_CONV = 4
LANES = 128
BF16_ROWS = 16
MOD_ROWS = 8
W_ALIGN = 32
HEAD_INTERLEAVE = 2
VMEM_LIMIT = 56 * 1024 * 1024


def _params(n_axes):
    return pltpu.CompilerParams(dimension_semantics=("arbitrary",) * n_axes,
                                vmem_limit_bytes=VMEM_LIMIT)


def _dot(a, b):
    return jnp.dot(a, b, preferred_element_type=F32)


def _dot_nt(a, b):
    return lax.dot_general(a, b, (((1,), (1,)), ((), ())), preferred_element_type=F32)


def _dot_tn(a, b):
    return lax.dot_general(a, b, (((0,), (0,)), ((), ())), preferred_element_type=F32)


def _cumsum_time(tri, x):
    hi = x.astype(BF16)
    r1 = x - hi.astype(F32)
    mid = r1.astype(BF16)
    lo = (r1 - mid.astype(F32)).astype(BF16)
    return _dot(tri, hi) + _dot(tri, mid) + _dot(tri, lo)


def _sigmoid(x):
    return 1.0 / (1.0 + jnp.exp(-x))


def _silu(x):
    return x * _sigmoid(x)


def _softplus(x):
    return jnp.maximum(x, 0.0) + jnp.log(1.0 + jnp.exp(-jnp.abs(x)))


def _time_masks(reverse):
    jj = lax.broadcasted_iota(jnp.int32, (CHUNK, CHUNK), 0)
    ss = lax.broadcasted_iota(jnp.int32, (CHUNK, CHUNK), 1)
    mask = (ss >= jj) if reverse else (ss <= jj)
    return mask, jnp.where(mask, 1.0, 0.0).astype(BF16)


def _mm_kernel(a_ref, w_ref, o_ref):
    o_ref[...] = _dot(a_ref[...].astype(BF16), w_ref[...].astype(BF16)).astype(o_ref.dtype)


def _matmul(a, w, out_dtype, tm, tn, row_off=0, rows=None, col_off=0, cols=None, name="matmul"):
    k = a.shape[1]
    rows = a.shape[0] - row_off if rows is None else rows
    cols = w.shape[1] - col_off if cols is None else cols
    assert rows % tm == 0 and cols % tn == 0 and row_off % tm == 0 and col_off % tn == 0
    roff, coff = row_off // tm, col_off // tn
    return pl.pallas_call(
        _mm_kernel,
        grid=(rows // tm, cols // tn),
        in_specs=[pl.BlockSpec((tm, k), lambda i, j: (i + roff, 0)),
                  pl.BlockSpec((k, tn), lambda i, j: (0, j + coff))],
        out_specs=pl.BlockSpec((tm, tn), lambda i, j: (i, j)),
        out_shape=jax.ShapeDtypeStruct((rows, cols), out_dtype),
        compiler_params=_params(2), name=name)(a, w)


def _wprep_kernel(off_ref, w_ref, o_ref, *, q_blocks, q_scale):
    del off_ref
    scale = jnp.where(pl.program_id(0) < q_blocks, q_scale, 1.0)
    o_ref[...] = (w_ref[...] * scale).T.astype(BF16)


def _wprep(wt, offsets, tn, q_blocks, q_scale):
    k = wt.shape[1]
    nblk = len(offsets)
    assert all(o % W_ALIGN == 0 for o in offsets)
    return pl.pallas_call(
        functools.partial(_wprep_kernel, q_blocks=q_blocks, q_scale=q_scale),
        grid_spec=pltpu.PrefetchScalarGridSpec(
            num_scalar_prefetch=1, grid=(nblk,),
            in_specs=[pl.BlockSpec((pl.Element(tn), pl.Element(k)),
                                   lambda j, off: (pl.multiple_of(off[j], W_ALIGN), 0))],
            out_specs=pl.BlockSpec((k, tn), lambda j, off: (0, j))),
        out_shape=jax.ShapeDtypeStruct((k, nblk * tn), BF16),
        compiler_params=_params(1), name="w_in_prep")(jnp.asarray(offsets, jnp.int32), wt)


def _ada_kernel(c_ref, w_ref, b_ref, o_ref):
    a = _silu(c_ref[...]).astype(BF16)
    o_ref[...] = _dot(a, w_ref[...].astype(BF16)) + b_ref[...]


def _ada(cvec, w, b, tn):
    k, n = w.shape
    return pl.pallas_call(
        _ada_kernel,
        grid=(n // tn,),
        in_specs=[pl.BlockSpec((MOD_ROWS, k), lambda j: (0, 0)),
                  pl.BlockSpec((k, tn), lambda j: (0, j)),
                  pl.BlockSpec((1, tn), lambda j: (0, j))],
        out_specs=pl.BlockSpec((MOD_ROWS, tn), lambda j: (0, j)),
        out_shape=jax.ShapeDtypeStruct((MOD_ROWS, n), F32),
        compiler_params=_params(1), name="ada")(cvec, w, b)


def _merge_kernel(ym_ref, ys_ref, wm_ref, ws_ref, gm_ref, gs_ref, o_ref):
    am = _dot(ym_ref[...], wm_ref[...])
    a_s = _dot(ys_ref[...], ws_ref[...])
    o_ref[...] = (_sigmoid(gm_ref[...].astype(F32)) * am
                  + _sigmoid(gs_ref[...].astype(F32)) * a_s).astype(o_ref.dtype)


def _merge(ym, ys, wm, ws, gates, gm_col, gs_col, tm, tn):
    m, k = ym.shape
    n = wm.shape[1]
    assert gm_col % tn == 0 and gs_col % tn == 0
    a_spec = pl.BlockSpec((tm, k), lambda i, j: (i, 0))
    w_spec = pl.BlockSpec((k, tn), lambda i, j: (0, j))
    return pl.pallas_call(
        _merge_kernel,
        grid=(m // tm, n // tn),
        in_specs=[a_spec, a_spec, w_spec, w_spec,
                  pl.BlockSpec((tm, tn), lambda i, j: (i, j + gm_col // tn)),
                  pl.BlockSpec((tm, tn), lambda i, j: (i, j + gs_col // tn))],
        out_specs=pl.BlockSpec((tm, tn), lambda i, j: (i, j)),
        out_shape=jax.ShapeDtypeStruct((m, n), BF16),
        compiler_params=_params(2), name="merge")(ym, ys, wm, ws, gates, gates)


def _swiglu_kernel(a_ref, wg_ref, wu_ref, o_ref):
    a = a_ref[...]
    o_ref[...] = (_silu(_dot(a, wg_ref[...])) * _dot(a, wu_ref[...])).astype(o_ref.dtype)


def _swiglu_up(a, wg, wu, tm, tn):
    m, k = a.shape
    n = wg.shape[1]
    w_spec = pl.BlockSpec((k, tn), lambda i, j: (0, j))
    return pl.pallas_call(
        _swiglu_kernel,
        grid=(m // tm, n // tn),
        in_specs=[pl.BlockSpec((tm, k), lambda i, j: (i, 0)), w_spec, w_spec],
        out_specs=pl.BlockSpec((tm, tn), lambda i, j: (i, j)),
        out_shape=jax.ShapeDtypeStruct((m, n), BF16),
        compiler_params=_params(2), name="swiglu_up")(a, wg, wu)


def _rms(x, g):
    return x * lax.rsqrt(jnp.mean(x * x, axis=-1, keepdims=True) + EPS) * g


def _norm_mod_kernel(xl_ref, xc_ref, g_ref, sh_ref, sc_ref, o_ref, *, n_lat_blocks):
    i = pl.program_id(0)

    def emit(x):
        o_ref[...] = (_rms(x, g_ref[...]) * (1.0 + sc_ref[0]) + sh_ref[0]).astype(o_ref.dtype)

    @pl.when(i < n_lat_blocks)
    def _():
        emit(xl_ref[...])

    @pl.when(i >= n_lat_blocks)
    def _():
        emit(xc_ref[...])


def _norm_mod(xl, xc, g, mod3, batch, tm):
    d = xl.shape[1]
    nlb = xl.shape[0] // tm
    ncb = xc.shape[0] // tm
    per_batch = nlb // batch

    def mod_row(i):
        return jnp.where(i < nlb, jnp.minimum(i, nlb - 1) // per_batch, batch)

    return pl.pallas_call(
        functools.partial(_norm_mod_kernel, n_lat_blocks=nlb),
        grid=(nlb + ncb,),
        in_specs=[pl.BlockSpec((tm, d), lambda i: (jnp.minimum(i, nlb - 1), 0)),
                  pl.BlockSpec((tm, d), lambda i: (jnp.maximum(i - nlb, 0), 0)),
                  pl.BlockSpec((1, d), lambda i: (0, 0)),
                  pl.BlockSpec((1, 1, d), lambda i: (mod_row(i) * 6 + 0, 0, 0)),
                  pl.BlockSpec((1, 1, d), lambda i: (mod_row(i) * 6 + 1, 0, 0))],
        out_specs=pl.BlockSpec((tm, d), lambda i: (i, 0)),
        out_shape=jax.ShapeDtypeStruct(((nlb + ncb) * tm, d), BF16),
        compiler_params=_params(1), name="norm_mod")(xl, xc, g, mod3, mod3)


def _resid_mid_kernel(mix_ref, x_ref, g1_ref, g2_ref, ga_ref, sh_ref, sc_ref, xn_ref, h_ref):
    xn = x_ref[...] + ga_ref[0] * _rms(mix_ref[...].astype(F32), g1_ref[...])
    xn_ref[...] = xn
    h_ref[...] = (_rms(xn, g2_ref[...]) * (1.0 + sc_ref[0]) + sh_ref[0]).astype(h_ref.dtype)


def _resid_mid(mix, x, g1, g2, mod3, batch, tm):
    m, d = x.shape
    per_batch = m // tm // batch
    row = pl.BlockSpec((tm, d), lambda i: (i, 0))
    vec = pl.BlockSpec((1, d), lambda i: (0, 0))

    def mod(which):
        return pl.BlockSpec((1, 1, d), lambda i: ((i // per_batch) * 6 + which, 0, 0))

    return pl.pallas_call(
        _resid_mid_kernel,
        grid=(m // tm,),
        in_specs=[row, row, vec, vec, mod(2), mod(3), mod(4)],
        out_specs=[row, row],
        out_shape=[jax.ShapeDtypeStruct((m, d), F32), jax.ShapeDtypeStruct((m, d), BF16)],
        compiler_params=_params(1), name="resid_mid")(mix, x, g1, g2, mod3, mod3, mod3)


def _resid_out_kernel(hf_ref, x_ref, g_ref, ga_ref, o_ref):
    o_ref[...] = x_ref[...] + ga_ref[0] * _rms(hf_ref[...].astype(F32), g_ref[...])


def _resid_out(hf, x, g, mod3, batch, tm):
    m, d = x.shape
    per_batch = m // tm // batch
    row = pl.BlockSpec((tm, d), lambda i: (i, 0))
    return pl.pallas_call(
        _resid_out_kernel,
        grid=(m // tm,),
        in_specs=[row, row, pl.BlockSpec((1, d), lambda i: (0, 0)),
                  pl.BlockSpec((1, 1, d), lambda i: ((i // per_batch) * 6 + 5, 0, 0))],
        out_specs=row,
        out_shape=jax.ShapeDtypeStruct((m, d), F32),
        compiler_params=_params(1), name="resid_out")(hf, x, g, mod3)


def _mlstm_kernel(*refs, reverse, d, final, dk, dv):
    if final:
        (q_ref, k_ref, v_ref, g_ref, gb_ref, hb_ref, og_ref, hg_ref,
         o_ref, ct_sc, n_sc, m_sc) = refs
    else:
        q_ref, k_ref, v_ref, g_ref, gb_ref, o_ref, ct_sc, n_sc, m_sc = refs

    @pl.when(pl.program_id(1) == 0)
    def _():
        ct_sc[...] = jnp.zeros_like(ct_sc)
        n_sc[...] = jnp.zeros_like(n_sc)
        m_sc[...] = jnp.zeros_like(m_sc)

    mask, tri = _time_masks(reverse)
    last = 0 if reverse else CHUNK - 1

    g = g_ref[...] + gb_ref[...]
    log_f = jnp.minimum(g, 0.0) - jnp.log(1.0 + jnp.exp(-jnp.abs(g)))
    csum = _cumsum_time(tri, log_f)
    off_i = d * 2 * M_HEADS
    off_f = off_i + M_HEADS
    li_t = g if off_i == 0 else pltpu.roll(g, LANES - off_i, axis=1)
    b_t = pltpu.roll(csum, LANES - off_f, axis=1)
    r_t = (li_t - b_t).T

    def gate_stage(h):
        b_c = b_t[:, h:h + 1]
        logd = jnp.where(mask, b_c + r_t[h:h + 1, :], -jnp.inf)
        m_prev = m_sc[h:h + 1, 0:1]
        m_carry = b_c + m_prev
        m_row = jnp.maximum(m_carry, jnp.max(logd, axis=1, keepdims=True))
        return dict(b_c=b_c, li_c=li_t[:, h:h + 1], m_prev=m_prev, m_row=m_row,
                    d_mat=jnp.exp(logd - m_row), w_carry=jnp.exp(m_carry - m_row),
                    q=q_ref[:, h * dk:(h + 1) * dk], k=k_ref[:, h * dk:(h + 1) * dk],
                    v=v_ref[:, h * dv:(h + 1) * dv], ct=ct_sc[h], n_row=n_sc[h:h + 1, :])

    def out_stage(h, st):
        s = st["s"]
        num = _dot(s.astype(BF16), st["v"]) + st["w_carry"] * _dot(st["q"], st["ct"].astype(BF16))
        den = (jnp.sum(s, axis=1, keepdims=True)
               + st["w_carry"] * jnp.sum(st["q"].astype(F32) * st["n_row"], axis=1, keepdims=True))
        h_out = num / jnp.maximum(jnp.abs(den), jnp.exp(-st["m_row"]))
        cols = slice(h * dv, (h + 1) * dv)
        if final:
            hn = _rms(h_out + hb_ref[:, cols].astype(F32), hg_ref[:, cols])
            o_ref[:, cols] = (hn * _sigmoid(og_ref[:, cols].astype(F32))).astype(o_ref.dtype)
        else:
            o_ref[:, cols] = h_out.astype(o_ref.dtype)

    def state_stage(h, st):
        b_c, k, v = st["b_c"], st["k"], st["v"]
        m_end = st["m_row"][last:last + 1, :]
        b_end = b_c[last:last + 1, :]
        w_state = jnp.exp(b_end - b_c + st["li_c"] - m_end)
        decay = jnp.exp(b_end + st["m_prev"] - m_end)
        ct_sc[h] = decay * st["ct"] + _dot_tn(k, (w_state * v.astype(F32)).astype(BF16))
        n_sc[h:h + 1, :] = decay * st["n_row"] + jnp.sum(w_state * k.astype(F32), axis=0, keepdims=True)
        m_sc[h:h + 1, :] = jnp.broadcast_to(m_end, (1, LANES))

    for h0 in range(0, M_HEADS, HEAD_INTERLEAVE):
        heads = range(h0, h0 + HEAD_INTERLEAVE)
        st = {h: gate_stage(h) for h in heads}
        for h in heads:
            st[h]["s"] = _dot_nt(st[h]["q"], st[h]["k"]) * st[h]["d_mat"]
        for h in heads:
            out_stage(h, st[h])
        for h in heads:
            state_stage(h, st[h])


def _mlstm_pass(qkv, gates, gate_b, *, batch, ncc, nlc, d, reverse, extra=None):
    hv = qkv.shape[1] // 2
    hk = hv // 2
    dk, dv = hk // M_HEADS, hv // M_HEADS
    final = extra is not None

    def comb_idx(b, t):
        cc = (ncc - 1 - t) if reverse else t
        lc = (nlc - 1 - (t - ncc)) if reverse else (t - ncc)
        return jnp.where(t < ncc, batch * nlc + b * ncc + cc, b * nlc + lc)

    def lat_idx(b, t):
        tt = jnp.maximum(t - ncc, 0)
        return b * nlc + ((nlc - 1 - tt) if reverse else tt)

    lat_spec = pl.BlockSpec((CHUNK, hv), lambda b, t: (lat_idx(b, t), 0))
    in_specs = [pl.BlockSpec((CHUNK, hk), lambda b, t: (comb_idx(b, t), 0)),
                pl.BlockSpec((CHUNK, hk), lambda b, t: (comb_idx(b, t), 1)),
                pl.BlockSpec((CHUNK, hv), lambda b, t: (comb_idx(b, t), 1)),
                pl.BlockSpec((CHUNK, LANES), lambda b, t: (comb_idx(b, t), 0)),
                pl.BlockSpec((1, LANES), lambda b, t: (0, 0))]
    args = [qkv, qkv, qkv, gates, gate_b]
    if final:
        in_specs += [lat_spec, lat_spec, pl.BlockSpec((1, hv), lambda b, t: (0, 0))]
        args += list(extra)
    return pl.pallas_call(
        functools.partial(_mlstm_kernel, reverse=reverse, d=d, final=final, dk=dk, dv=dv),
        grid=(batch, ncc + nlc),
        in_specs=in_specs,
        out_specs=lat_spec,
        out_shape=jax.ShapeDtypeStruct((batch * nlc * CHUNK, hv), BF16),
        scratch_shapes=[pltpu.VMEM((M_HEADS, dk, dv), F32),
                        pltpu.VMEM((M_HEADS, dk), F32),
                        pltpu.VMEM((M_HEADS, LANES), F32)],
        compiler_params=_params(2), name="mlstm_final" if final else "mlstm_bwd")(*args)


def _store_col_major(o_ref, scr):
    rows, cols, _ = scr.shape
    step = BF16_ROWS
    for p in range(cols // step):
        piece = pltpu.einshape("rcd->crd", scr[:, p * step:(p + 1) * step, :])
        o_ref[0, p * step * rows:(p + 1) * step * rows, :] = piece.reshape(step * rows, piece.shape[-1])


def _conv_lat_kernel(x_ref, w_ref, b_ref, o_ref, scr):
    rows, cols = x_ref.shape[1], x_ref.shape[2]
    w = w_ref[...]
    w0, w1, w2, w3, bias = w[0:1, :], w[1:2, :], w[2:3, :], w[3:4, :], b_ref[...]
    cidx = lax.broadcasted_iota(jnp.int32, (cols, 1), 0)

    def ld(r):
        return x_ref[0, r].astype(F32)

    def from_prev_col(a):
        return jnp.where(cidx >= 1, pltpu.roll(a, 1, axis=0), 0.0)

    def from_next_col(a):
        return jnp.where(cidx <= cols - 2, pltpu.roll(a, cols - 1, axis=0), 0.0)

    def emit(r, prev, cur, nxt1, nxt2):
        scr[r] = _silu(w0 * prev + w1 * cur + w2 * nxt1 + w3 * nxt2 + bias).astype(scr.dtype)

    emit(0, from_prev_col(ld(rows - 1)), ld(0), ld(1), ld(2))

    def body(r, carry):
        emit(r, ld(r - 1), ld(r), ld(r + 1), ld(r + 2))
        return carry

    lax.fori_loop(1, rows - 2, body, 0)
    top0 = from_next_col(ld(0))
    emit(rows - 2, ld(rows - 3), ld(rows - 2), ld(rows - 1), top0)
    emit(rows - 1, ld(rows - 2), ld(rows - 1), top0, from_next_col(ld(1)))
    _store_col_major(o_ref, scr)


def _conv_lat(x4, w, b, ch_off, ch_n, cb, t_all):
    assert ch_off % cb == 0 and ch_n % cb == 0
    off = ch_off // cb
    bsz, rows, cols, _ = x4.shape
    return pl.pallas_call(
        _conv_lat_kernel,
        grid=(bsz, ch_n // cb),
        in_specs=[pl.BlockSpec((1, rows, cols, cb), lambda b_, j: (b_, 0, 0, j + off)),
                  pl.BlockSpec((S_CONV, cb), lambda b_, j: (0, j + off)),
                  pl.BlockSpec((1, cb), lambda b_, j: (0, j + off))],
        out_specs=pl.BlockSpec((1, rows * cols, cb), lambda b_, j: (b_, 0, j)),
        out_shape=jax.ShapeDtypeStruct((bsz, t_all, ch_n), BF16),
        scratch_shapes=[pltpu.VMEM((rows, cols, cb), BF16)],
        compiler_params=_params(2), name="conv_lat")(x4, w, b)


def _conv_ctx_kernel(x_ref, w_ref, b_ref, dst_ref, o_ref):
    del dst_ref
    tlen = x_ref.shape[1]
    w = w_ref[...]
    x = x_ref[0].astype(F32)
    tidx = lax.broadcasted_iota(jnp.int32, (tlen, 1), 0)
    prev = jnp.where(tidx >= 1, pltpu.roll(x, 1, axis=0), 0.0)
    nxt1 = jnp.where(tidx <= tlen - 2, pltpu.roll(x, tlen - 1, axis=0), 0.0)
    nxt2 = jnp.where(tidx <= tlen - 3, pltpu.roll(x, tlen - 2, axis=0), 0.0)
    o_ref[0] = _silu(w[0:1, :] * prev + w[1:2, :] * x + w[2:3, :] * nxt1 + w[3:4, :] * nxt2
                     + b_ref[...]).astype(o_ref.dtype)


def _conv_ctx(x3, w, b, dst, ch_off, cb, t_lat):
    bsz, tlen, _ = x3.shape
    ch_n = dst.shape[2]
    assert ch_off % cb == 0 and ch_n % cb == 0 and t_lat % tlen == 0
    off = ch_off // cb
    return pl.pallas_call(
        _conv_ctx_kernel,
        grid=(bsz, ch_n // cb),
        in_specs=[pl.BlockSpec((1, tlen, cb), lambda b_, j: (b_, 0, j + off)),
                  pl.BlockSpec((S_CONV, cb), lambda b_, j: (0, j + off)),
                  pl.BlockSpec((1, cb), lambda b_, j: (0, j + off)),
                  pl.BlockSpec(memory_space=pl.ANY)],
        out_specs=pl.BlockSpec((1, tlen, cb), lambda b_, j: (b_, t_lat // tlen, j)),
        out_shape=jax.ShapeDtypeStruct(dst.shape, dst.dtype),
        input_output_aliases={3: 0},
        compiler_params=_params(2), name="conv_ctx")(x3, w, b, dst)


def _dt_lat_kernel(x_ref, o_ref):
    _store_col_major(o_ref, x_ref.at[0])


def _dt_ctx_kernel(x_ref, dst_ref, o_ref):
    del dst_ref
    o_ref[...] = x_ref[...]


def _dt_layout(dt_lat4, dt_ctx3, t_all):
    bsz, rows, cols, n = dt_lat4.shape
    tlen = dt_ctx3.shape[1]
    t_lat = rows * cols
    out = pl.pallas_call(
        _dt_lat_kernel,
        grid=(bsz,),
        in_specs=[pl.BlockSpec((1, rows, cols, n), lambda b_: (b_, 0, 0, 0))],
        out_specs=pl.BlockSpec((1, t_lat, n), lambda b_: (b_, 0, 0)),
        out_shape=jax.ShapeDtypeStruct((bsz, t_all, n), F32),
        compiler_params=_params(1), name="dt_lat")(dt_lat4)
    return pl.pallas_call(
        _dt_ctx_kernel,
        grid=(bsz,),
        in_specs=[pl.BlockSpec((1, tlen, n), lambda b_: (b_, 0, 0)),
                  pl.BlockSpec(memory_space=pl.ANY)],
        out_specs=pl.BlockSpec((1, tlen, n), lambda b_: (b_, t_lat // tlen, 0)),
        out_shape=jax.ShapeDtypeStruct(out.shape, out.dtype),
        input_output_aliases={1: 0},
        compiler_params=_params(1), name="dt_ctx")(dt_ctx3, out)


def _split2(x):
    hi = x.astype(BF16)
    return hi, (x - hi.astype(F32)).astype(BF16)


def _head_selector(first, hpg):
    gw = hpg * S_HEAD_DIM
    row = lax.broadcasted_iota(jnp.int32, (LANES, gw), 0)
    head = lax.broadcasted_iota(jnp.int32, (LANES, gw), 1) // S_HEAD_DIM
    return jnp.where(row == first + head, 1.0, 0.0).astype(BF16)


def _expand_heads(stack, sel, parts):
    full = _dot(stack, sel)
    out, r0 = [], 0
    for rows in parts:
        out.append(full[r0:r0 + rows] + full[r0 + rows:r0 + 2 * rows])
        r0 += 2 * rows
    return out


def _ssd_kernel(*refs, reverse, col0, final, hpg):
    if final:
        (x_ref, b_ref, c_ref, dt_ref, dtb_ref, alog_ref, yb_ref, skip_ref, o_ref, h_sc) = refs
    else:
        (x_ref, b_ref, c_ref, dt_ref, dtb_ref, alog_ref, o_ref, h_sc) = refs
    gw = hpg * S_HEAD_DIM

    @pl.when(pl.program_id(1) == 0)
    def _():
        h_sc[...] = jnp.zeros_like(h_sc)

    mask, tri = _time_masks(reverse)
    last = 0 if reverse else CHUNK - 1
    lane_head = lax.broadcasted_iota(jnp.int32, (CHUNK, gw), 1) // S_HEAD_DIM

    dt = _softplus(dt_ref[0] + dtb_ref[...])
    cum = _cumsum_time(tri, -dt * jnp.exp(alog_ref[...]))
    cum_t = cum.T
    cum_end = cum[last:last + 1, :]
    e_end = jnp.broadcast_to(jnp.exp(cum_end), (BF16_ROWS, LANES))
    factors = (dt, jnp.exp(cum), jnp.exp(cum_end - cum), e_end)
    stack = jnp.concatenate([p for f in factors for p in _split2(f)], axis=0)
    parts = tuple(f.shape[0] for f in factors)

    for g in range(S_GROUPS):
        first = col0 + g * hpg
        cols = slice(g * gw, (g + 1) * gw)
        xg = x_ref[0, :, cols].astype(F32)
        bg = b_ref[0, :, g * S_STATE:(g + 1) * S_STATE]
        cg = c_ref[0, :, g * S_STATE:(g + 1) * S_STATE]
        cb = _dot_nt(cg, bg)
        ws = []
        for r in range(hpg):
            idx = first + r
            seg = jnp.where(mask, cum[:, idx:idx + 1] - cum_t[idx:idx + 1, :], -jnp.inf)
            ws.append((cb * jnp.exp(seg)).astype(BF16))
        w_cat = jnp.concatenate(ws, axis=1)
        dt_x, e_cum_x, e_rest_x, e_end_x = _expand_heads(stack, _head_selector(first, hpg), parts)
        xdt = xg * dt_x
        xdt_b = xdt.astype(BF16)
        bd = jnp.concatenate([jnp.where(lane_head == r, xdt_b, jnp.zeros_like(xdt_b))
                              for r in range(hpg)], axis=0)
        h_t = h_sc[g]
        y = _dot(w_cat, bd) + e_cum_x * _dot(cg, h_t.astype(BF16))
        wx = (xdt * e_rest_x).astype(BF16)
        h_sc[g] = e_end_x[0:1] * h_t + _dot_tn(bg, wx)
        if final:
            y = y + yb_ref[0, :, cols].astype(F32) + skip_ref[:, cols] * xg
        o_ref[0, :, cols] = y.astype(o_ref.dtype)


def _ssd_pass(xs, bm, cm, dt, dt_bias, a_log, *, ncc, nlc, col0, reverse, hpg, extra=None):
    bsz, _, ci = xs.shape
    sbc = S_GROUPS * S_STATE
    final = extra is not None

    def any_idx(b, t):
        cc = (ncc - 1 - t) if reverse else t
        lc = (nlc - 1 - (t - ncc)) if reverse else (t - ncc)
        return (b, jnp.where(t < ncc, nlc + cc, lc), 0)

    def lat_idx(b, t):
        tt = jnp.maximum(t - ncc, 0)
        return (b, (nlc - 1 - tt) if reverse else tt, 0)

    vec = pl.BlockSpec((1, LANES), lambda b, t: (0, 0))
    in_specs = [pl.BlockSpec((1, CHUNK, ci), any_idx), pl.BlockSpec((1, CHUNK, sbc), any_idx),
                pl.BlockSpec((1, CHUNK, sbc), any_idx), pl.BlockSpec((1, CHUNK, LANES), any_idx),
                vec, vec]
    args = [xs, bm, cm, dt, dt_bias, a_log]
    if final:
        in_specs += [pl.BlockSpec((1, CHUNK, ci), lat_idx), pl.BlockSpec((1, ci), lambda b, t: (0, 0))]
        args += list(extra)
    return pl.pallas_call(
        functools.partial(_ssd_kernel, reverse=reverse, col0=col0, final=final, hpg=hpg),
        grid=(bsz, ncc + nlc),
        in_specs=in_specs,
        out_specs=pl.BlockSpec((1, CHUNK, ci), lat_idx),
        out_shape=jax.ShapeDtypeStruct((bsz, nlc * CHUNK, ci), BF16),
        scratch_shapes=[pltpu.VMEM((S_GROUPS, S_STATE, hpg * S_HEAD_DIM), F32)],
        compiler_params=_params(2), name="ssd_final" if final else "ssd_bwd")(*args)


def _ssd_out_kernel(y_ref, z_ref, g_ref, o_ref):
    y = pltpu.einshape("crd->rcd", y_ref[0]).astype(F32)
    yt = y * _silu(z_ref[0].astype(F32))
    o_ref[0] = _rms(yt, g_ref[...]).astype(o_ref.dtype)


def _ssd_out(y_cm, z4, z_col, norm_g, gw):
    bsz, cols, rows, ci = y_cm.shape
    rb = BF16_ROWS
    assert z_col % gw == 0 and rows % rb == 0
    zoff = z_col // gw
    return pl.pallas_call(
        _ssd_out_kernel,
        grid=(bsz, rows // rb, ci // gw),
        in_specs=[pl.BlockSpec((1, cols, rb, gw), lambda b, r, g: (b, 0, r, g)),
                  pl.BlockSpec((1, rb, cols, gw), lambda b, r, g: (b, r, 0, g + zoff)),
                  pl.BlockSpec((1, gw), lambda b, r, g: (0, g))],
        out_specs=pl.BlockSpec((1, rb, cols, gw), lambda b, r, g: (b, r, 0, g)),
        out_shape=jax.ShapeDtypeStruct((bsz, rows, cols, ci), BF16),
        compiler_params=_params(3), name="ssd_out")(y_cm, z4, norm_g)


def _pad_cols(a, n):
    return jnp.pad(a, ((0, 0), (0, n - a.shape[1])))


def kernel(x, c, ctx, c_ctx, w_ada, b_ada, norm_g, w_in, m_gate_b, m_norm_g, s_conv_w, s_conv_b,
           s_dt_bias, s_a_log, s_d, s_norm_g, w_bm, w_bs, w_out, w_ffn_gate, w_ffn_up, w_ffn_down):
    batch, seq, d_model = x.shape
    ctx_len = ctx.shape[1]
    depth = w_in.shape[0]
    rows = seq // GRID_W
    assert depth == 1 and rows * 2 == CHUNK and batch + 1 <= MOD_ROWS
    assert ctx_len % CHUNK == 0 and seq % CHUNK == 0
    m_qk = d_model // 2
    m_v = d_model
    s_inner = d_model
    s_heads = s_inner // S_HEAD_DIM
    hpg = s_heads // S_GROUPS
    gw = hpg * S_HEAD_DIM
    s_bc = S_GROUPS * S_STATE
    conv_ch = s_inner + 2 * s_bc
    n_gates = N_DIR * 2 * M_HEADS
    assert N_DIR * s_heads <= LANES and n_gates <= LANES and hpg % 2 == 0
    ncc, nlc = ctx_len // CHUNK, seq // CHUNK
    n_ctx, n_lat = batch * ctx_len, batch * seq
    t_all = seq + ctx_len
    li = 0

    cvec = jnp.zeros((MOD_ROWS, d_model), F32).at[:batch].set(c).at[batch].set(c_ctx)
    mod = _ada(cvec, w_ada[li], b_ada[li][None, :], tn=512)
    mod3 = mod.reshape(MOD_ROWS * 6, 1, d_model)
    g_n = norm_g[li]

    u = _norm_mod(x.reshape(n_lat, d_model), ctx.reshape(n_ctx, d_model), g_n[0][None, :], mod3,
                  batch, tm=min(256, n_ctx))

    wt = w_in[li].T
    tn = 512
    src, o0 = {}, 0
    for name, n in (("q", m_qk), ("k", m_qk), ("v", m_v), ("gates", n_gates), ("xbc", conv_ch),
                    ("dt", N_DIR * s_heads), ("o", m_v), ("z", s_inner), ("gm", d_model), ("gs", d_model)):
        src[name] = (o0, n)
        o0 += n
    packed = ("q", "k", "v", "xbc", "o", "z", "gm", "gs")
    offsets, dst, p0 = [], {}, 0
    for name in packed:
        s0, n = src[name]
        assert n % tn == 0
        offsets += [s0 + i * tn for i in range(n // tn)]
        dst[name] = p0
        p0 += n
    w_pk = _wprep(wt, offsets, tn, q_blocks=m_qk // tn, q_scale=float(m_qk // M_HEADS) ** -0.5)
    w_aux = _wprep(wt, [src["gates"][0], src["dt"][0]], LANES, q_blocks=0, q_scale=1.0)

    tm_tok = min(1024, n_ctx)
    tn_big = 1024
    qkv = _matmul(u, w_pk, BF16, tm_tok, tn_big, 0, None, dst["q"], 2 * m_qk + m_v, name="proj_qkv")
    xbc_l = _matmul(u, w_pk, BF16, tm_tok, tn_big, 0, n_lat, dst["xbc"], conv_ch, name="proj_xbc_lat")
    xbc_c = _matmul(u, w_pk, BF16, tm_tok, tn_big, n_lat, n_ctx, dst["xbc"], conv_ch, name="proj_xbc_ctx")
    ozg = _matmul(u, w_pk, BF16, tm_tok, tn_big, 0, n_lat, dst["o"], m_v + s_inner + 2 * d_model,
                  name="proj_ozg")
    gates = _matmul(u, w_aux, F32, tm_tok, LANES, 0, None, 0, LANES, name="proj_gates")
    dt_l = _matmul(u, w_aux, F32, tm_tok, LANES, 0, n_lat, LANES, LANES, name="proj_dt_lat")
    dt_c = _matmul(u, w_aux, F32, tm_tok, LANES, n_lat, n_ctx, LANES, LANES, name="proj_dt_ctx")

    gate_b = _pad_cols(m_gate_b[li].reshape(1, -1).astype(F32), LANES)
    common = dict(batch=batch, ncc=ncc, nlc=nlc)
    h_bwd = _mlstm_pass(qkv, gates, gate_b, d=1, reverse=True, **common)
    y_m = _mlstm_pass(qkv, gates, gate_b, d=0, reverse=False,
                      extra=(h_bwd, ozg, m_norm_g[li][None, :].astype(F32)), **common)

    cw, cbias = s_conv_w[li].astype(F32), s_conv_b[li][None, :].astype(F32)
    xl4 = xbc_l.reshape(batch, rows, GRID_W, conv_ch)
    xc3 = xbc_c.reshape(batch, ctx_len, conv_ch)
    cb_sz = 512
    scan_in = []
    for o_, n_ in ((0, s_inner), (s_inner, s_bc), (s_inner + s_bc, s_bc)):
        part = _conv_lat(xl4, cw, cbias, o_, n_, cb_sz, t_all)
        scan_in.append(_conv_ctx(xc3, cw, cbias, part, o_, cb_sz, seq))
    scan_in.append(_dt_layout(dt_l.reshape(batch, rows, GRID_W, LANES),
                              dt_c.reshape(batch, ctx_len, LANES), t_all))
    dtb = _pad_cols(s_dt_bias[li].reshape(1, -1).astype(F32), LANES)
    alog = _pad_cols(s_a_log[li].reshape(1, -1).astype(F32), LANES)
    skip = jnp.repeat(s_d[li].astype(F32), S_HEAD_DIM)[None, :]
    scan = dict(ncc=ncc, nlc=nlc, hpg=hpg)
    y_bwd = _ssd_pass(*scan_in, dtb, alog, col0=s_heads, reverse=True, **scan)
    y_tot = _ssd_pass(*scan_in, dtb, alog, col0=0, reverse=False, extra=(y_bwd, skip), **scan)
    y_s = _ssd_out(y_tot.reshape(batch, GRID_W, rows, s_inner),
                   ozg.reshape(batch, rows, GRID_W, ozg.shape[1]), dst["z"] - dst["o"],
                   s_norm_g[li][None, :].astype(F32), gw).reshape(n_lat, s_inner)

    tm_lat = min(1024, n_lat)
    mix_pre = _merge(y_m, y_s, w_bm[li].astype(BF16), w_bs[li].astype(BF16), ozg,
                     dst["gm"] - dst["o"], dst["gs"] - dst["o"], tm_lat, 256)
    mix = _matmul(mix_pre, w_out[li].astype(BF16), BF16, tm_lat, tn_big, name="out_proj")
    x2 = x.reshape(n_lat, d_model)
    x_new, h_mod = _resid_mid(mix, x2, g_n[1][None, :], g_n[2][None, :], mod3, batch, tm=128)

    act = _swiglu_up(h_mod, w_ffn_gate[li].astype(BF16), w_ffn_up[li].astype(BF16),
                     min(2048, n_lat), 256)
    hf = _matmul(act, w_ffn_down[li].astype(BF16), BF16, min(512, n_lat), 512, name="ffn_down")
    out = _resid_out(hf, x_new, g_n[3][None, :], mod3, batch, tm=128)
    return out.reshape(batch, seq, d_model)
```

```python
import functools

import jax
import jax.numpy as jnp
from jax import lax
from jax.experimental import pallas as pl
from jax.experimental.pallas import tpu as pltpu

F32 = jnp.float32
BF16 = jnp.bfloat16

CHUNK = 128
GRID_W = 64
EPS = 1e-6
N_DIR = 2
M_HEADS = 8
S_GROUPS = 8
S_HEAD_DIM = 64
S_STATE = 128
S_CONV = 4
LANES = 128
BF16_ROWS = 16
MOD_ROWS = 8
W_ALIGN = 32
HEAD_INTERLEAVE = 2
VMEM_LIMIT = 56 * 1024 * 1024


def _params(n_axes):
    return pltpu.CompilerParams(dimension_semantics=("arbitrary",) * n_axes,
                                vmem_limit_bytes=VMEM_LIMIT)


def _dot(a, b):
    return jnp.dot(a, b, preferred_element_type=F32)


def _dot_nt(a, b):
    return lax.dot_general(a, b, (((1,), (1,)), ((), ())), preferred_element_type=F32)


def _dot_tn(a, b):
    return lax.dot_general(a, b, (((0,), (0,)), ((), ())), preferred_element_type=F32)


def _cumsum_time(tri, x):
    hi = x.astype(BF16)
    r1 = x - hi.astype(F32)
    mid = r1.astype(BF16)
    lo = (r1 - mid.astype(F32)).astype(BF16)
    return _dot(tri, hi) + _dot(tri, mid) + _dot(tri, lo)


def _sigmoid(x):
    return 1.0 / (1.0 + jnp.exp(-x))


def _silu(x):
    return x * _sigmoid(x)


def _softplus(x):
    return jnp.maximum(x, 0.0) + jnp.log(1.0 + jnp.exp(-jnp.abs(x)))


def _time_masks(reverse):
    jj = lax.broadcasted_iota(jnp.int32, (CHUNK, CHUNK), 0)
    ss = lax.broadcasted_iota(jnp.int32, (CHUNK, CHUNK), 1)
    mask = (ss >= jj) if reverse else (ss <= jj)
    return mask, jnp.where(mask, 1.0, 0.0).astype(BF16)


def _mm_kernel(a_ref, w_ref, o_ref):
    o_ref[...] = _dot(a_ref[...].astype(BF16), w_ref[...].astype(BF16)).astype(o_ref.dtype)


def _matmul(a, w, out_dtype, tm, tn, row_off=0, rows=None, col_off=0, cols=None, name="matmul"):
    k = a.shape[1]
    rows = a.shape[0] - row_off if rows is None else rows
    cols = w.shape[1] - col_off if cols is None else cols
    assert rows % tm == 0 and cols % tn == 0 and row_off % tm == 0 and col_off % tn == 0
    roff, coff = row_off // tm, col_off // tn
    return pl.pallas_call(
        _mm_kernel,
        grid=(rows // tm, cols // tn),
        in_specs=[pl.BlockSpec((tm, k), lambda i, j: (i + roff, 0)),
                  pl.BlockSpec((k, tn), lambda i, j: (0, j + coff))],
        out_specs=pl.BlockSpec((tm, tn), lambda i, j: (i, j)),
        out_shape=jax.ShapeDtypeStruct((rows, cols), out_dtype),
        compiler_params=_params(2), name=name)(a, w)


def _wprep_kernel(off_ref, w_ref, o_ref, *, q_blocks, q_scale):
    del off_ref
    scale = jnp.where(pl.program_id(0) < q_blocks, q_scale, 1.0)
    o_ref[...] = (w_ref[...] * scale).T.astype(BF16)


def _wprep(wt, offsets, tn, q_blocks, q_scale):
    k = wt.shape[1]
    nblk = len(offsets)
    assert all(o % W_ALIGN == 0 for o in offsets)
    return pl.pallas_call(
        functools.partial(_wprep_kernel, q_blocks=q_blocks, q_scale=q_scale),
        grid_spec=pltpu.PrefetchScalarGridSpec(
            num_scalar_prefetch=1, grid=(nblk,),
            in_specs=[pl.BlockSpec((pl.Element(tn), pl.Element(k)),
                                   lambda j, off: (pl.multiple_of(off[j], W_ALIGN), 0))],
            out_specs=pl.BlockSpec((k, tn), lambda j, off: (0, j))),
        out_shape=jax.ShapeDtypeStruct((k, nblk * tn), BF16),
        compiler_params=_params(1), name="w_in_prep")(jnp.asarray(offsets, jnp.int32), wt)


def _ada_kernel(c_ref, w_ref, b_ref, o_ref):
    a = _silu(c_ref[...]).astype(BF16)
    o_ref[...] = _dot(a, w_ref[...].astype(BF16)) + b_ref[...]


def _ada(cvec, w, b, tn):
    k, n = w.shape
    return pl.pallas_call(
        _ada_kernel,
        grid=(n // tn,),
        in_specs=[pl.BlockSpec((MOD_ROWS, k), lambda j: (0, 0)),
                  pl.BlockSpec((k, tn), lambda j: (0, j)),
                  pl.BlockSpec((1, tn), lambda j: (0, j))],
        out_specs=pl.BlockSpec((MOD_ROWS, tn), lambda j: (0, j)),
        out_shape=jax.ShapeDtypeStruct((MOD_ROWS, n), F32),
        compiler_params=_params(1), name="ada")(cvec, w, b)


def _merge_kernel(ym_ref, ys_ref, wm_ref, ws_ref, gm_ref, gs_ref, o_ref):
    am = _dot(ym_ref[...], wm_ref[...])
    a_s = _dot(ys_ref[...], ws_ref[...])
    o_ref[...] = (_sigmoid(gm_ref[...].astype(F32)) * am
                  + _sigmoid(gs_ref[...].astype(F32)) * a_s).astype(o_ref.dtype)


def _merge(ym, ys, wm, ws, gates, tm, tn):
    m, k = ym.shape
    n = wm.shape[1]
    assert gates.shape[1] == 2 * n
    a_spec = pl.BlockSpec((tm, k), lambda i, j: (i, 0))
    w_spec = pl.BlockSpec((k, tn), lambda i, j: (0, j))
    return pl.pallas_call(
        _merge_kernel,
        grid=(m // tm, n // tn),
        in_specs=[a_spec, a_spec, w_spec, w_spec,
                  pl.BlockSpec((tm, tn), lambda i, j: (i, j)),
                  pl.BlockSpec((tm, tn), lambda i, j: (i, j + n // tn))],
        out_specs=pl.BlockSpec((tm, tn), lambda i, j: (i, j)),
        out_shape=jax.ShapeDtypeStruct((m, n), BF16),
        compiler_params=_params(2), name="merge")(ym, ys, wm, ws, gates, gates)


def _swiglu_kernel(a_ref, wg_ref, wu_ref, o_ref):
    a = a_ref[...]
    o_ref[...] = (_silu(_dot(a, wg_ref[...])) * _dot(a, wu_ref[...])).astype(o_ref.dtype)


def _swiglu_up(a, wg, wu, tm, tn):
    m, k = a.shape
    n = wg.shape[1]
    w_spec = pl.BlockSpec((k, tn), lambda i, j: (0, j))
    return pl.pallas_call(
        _swiglu_kernel,
        grid=(m // tm, n // tn),
        in_specs=[pl.BlockSpec((tm, k), lambda i, j: (i, 0)), w_spec, w_spec],
        out_specs=pl.BlockSpec((tm, tn), lambda i, j: (i, j)),
        out_shape=jax.ShapeDtypeStruct((m, n), BF16),
        compiler_params=_params(2), name="swiglu_up")(a, wg, wu)


def _rms(x, g):
    return x * lax.rsqrt(jnp.mean(x * x, axis=-1, keepdims=True) + EPS) * g


def _norm_mod_kernel(xl_ref, xc_ref, g_ref, sh_ref, sc_ref, o_ref, *, n_lat_blocks):
    i = pl.program_id(0)

    def emit(x):
        o_ref[...] = (_rms(x, g_ref[...]) * (1.0 + sc_ref[0]) + sh_ref[0]).astype(o_ref.dtype)

    @pl.when(i < n_lat_blocks)
    def _():
        emit(xl_ref[...])

    @pl.when(i >= n_lat_blocks)
    def _():
        emit(xc_ref[...])


def _norm_mod(xl, xc, g, mod3, batch, tm):
    d = xl.shape[1]
    nlb = xl.shape[0] // tm
    ncb = xc.shape[0] // tm
    per_batch = nlb // batch

    def mod_row(i):
        return jnp.where(i < nlb, jnp.minimum(i, nlb - 1) // per_batch, batch)

    return pl.pallas_call(
        functools.partial(_norm_mod_kernel, n_lat_blocks=nlb),
        grid=(nlb + ncb,),
        in_specs=[pl.BlockSpec((tm, d), lambda i: (jnp.minimum(i, nlb - 1), 0)),
                  pl.BlockSpec((tm, d), lambda i: (jnp.maximum(i - nlb, 0), 0)),
                  pl.BlockSpec((1, d), lambda i: (0, 0)),
                  pl.BlockSpec((1, 1, d), lambda i: (mod_row(i) * 6 + 0, 0, 0)),
                  pl.BlockSpec((1, 1, d), lambda i: (mod_row(i) * 6 + 1, 0, 0))],
        out_specs=pl.BlockSpec((tm, d), lambda i: (i, 0)),
        out_shape=jax.ShapeDtypeStruct(((nlb + ncb) * tm, d), BF16),
        compiler_params=_params(1), name="norm_mod")(xl, xc, g, mod3, mod3)


def _resid_mid_kernel(mix_ref, x_ref, g1_ref, g2_ref, ga_ref, sh_ref, sc_ref, xn_ref, h_ref):
    xn = x_ref[...] + ga_ref[0] * _rms(mix_ref[...].astype(F32), g1_ref[...])
    xn_ref[...] = xn
    h_ref[...] = (_rms(xn, g2_ref[...]) * (1.0 + sc_ref[0]) + sh_ref[0]).astype(h_ref.dtype)


def _resid_mid(mix, x, g1, g2, mod3, batch, tm):
    m, d = x.shape
    per_batch = m // tm // batch
    row = pl.BlockSpec((tm, d), lambda i: (i, 0))
    vec = pl.BlockSpec((1, d), lambda i: (0, 0))

    def mod(which):
        return pl.BlockSpec((1, 1, d), lambda i: ((i // per_batch) * 6 + which, 0, 0))

    return pl.pallas_call(
        _resid_mid_kernel,
        grid=(m // tm,),
        in_specs=[row, row, vec, vec, mod(2), mod(3), mod(4)],
        out_specs=[row, row],
        out_shape=[jax.ShapeDtypeStruct((m, d), F32), jax.ShapeDtypeStruct((m, d), BF16)],
        compiler_params=_params(1), name="resid_mid")(mix, x, g1, g2, mod3, mod3, mod3)


def _resid_out_kernel(hf_ref, x_ref, g_ref, ga_ref, o_ref):
    o_ref[...] = x_ref[...] + ga_ref[0] * _rms(hf_ref[...].astype(F32), g_ref[...])


def _resid_out(hf, x, g, mod3, batch, tm):
    m, d = x.shape
    per_batch = m // tm // batch
    row = pl.BlockSpec((tm, d), lambda i: (i, 0))
    return pl.pallas_call(
        _resid_out_kernel,
        grid=(m // tm,),
        in_specs=[row, row, pl.BlockSpec((1, d), lambda i: (0, 0)),
                  pl.BlockSpec((1, 1, d), lambda i: ((i // per_batch) * 6 + 5, 0, 0))],
        out_specs=row,
        out_shape=jax.ShapeDtypeStruct((m, d), F32),
        compiler_params=_params(1), name="resid_out")(hf, x, g, mod3)


SIDE_PIECES = 4


class _Side:
    def __init__(self, a, w, rows, col_off, cols, tm, tn, steps_per_batch, n_steps):
        assert rows % tm == 0 and cols % tn == 0 and col_off % tn == 0
        self.a, self.w, self.rows, self.cols, self.tm, self.tn = a, w, rows, cols, tm, tn
        self.nj, self.coff, self.spb = cols // tn, col_off // tn, steps_per_batch
        self.n_tiles = (rows // tm) * self.nj
        assert self.n_tiles <= n_steps, (self.n_tiles, n_steps)

    def _tile(self, b, t):
        s = jnp.minimum(b * self.spb + t, self.n_tiles - 1)
        return s // self.nj, s % self.nj

    def in_specs(self):
        k = self.a.shape[1]
        return [pl.BlockSpec((self.tm, k), lambda b, t: (self._tile(b, t)[0], 0)),
                pl.BlockSpec((k, self.tn), lambda b, t: (0, self._tile(b, t)[1] + self.coff))]

    def out_spec(self):
        return pl.BlockSpec((self.tm, self.tn), lambda b, t: self._tile(b, t))

    def out_shape(self):
        return jax.ShapeDtypeStruct((self.rows, self.cols), BF16)


def _side_piece(a_ref, w_ref, o_ref, p):
    tm, tn = o_ref.shape
    rs = slice((p // 2) * (tm // 2), (p // 2 + 1) * (tm // 2))
    cs = slice((p % 2) * (tn // 2), (p % 2 + 1) * (tn // 2))
    o_ref[rs, cs] = _dot(a_ref[rs, :], w_ref[:, cs]).astype(o_ref.dtype)


def _mlstm_kernel(*refs, reverse, d, final, side, dk, dv):
    n_in = 5 + (3 if final else 0)
    scan_in, refs = refs[:n_in], refs[n_in:]
    if side:
        (sa_ref, sw_ref, o_ref, so_ref), refs = refs[:4], refs[4:]
    else:
        o_ref, refs = refs[0], refs[1:]
    ct_sc, n_sc, m_sc = refs
    q_ref, k_ref, v_ref, g_ref, gb_ref = scan_in[:5]
    if final:
        hb_ref, og_ref, hg_ref = scan_in[5:]

    @pl.when(pl.program_id(1) == 0)
    def _():
        ct_sc[...] = jnp.zeros_like(ct_sc)
        n_sc[...] = jnp.zeros_like(n_sc)
        m_sc[...] = jnp.zeros_like(m_sc)

    mask, tri = _time_masks(reverse)
    last = 0 if reverse else CHUNK - 1

    g = g_ref[...] + gb_ref[...]
    log_f = jnp.minimum(g, 0.0) - jnp.log(1.0 + jnp.exp(-jnp.abs(g)))
    csum = _cumsum_time(tri, log_f)
    off_i = d * 2 * M_HEADS
    off_f = off_i + M_HEADS
    li_t = g if off_i == 0 else pltpu.roll(g, LANES - off_i, axis=1)
    b_t = pltpu.roll(csum, LANES - off_f, axis=1)
    r_t = (li_t - b_t).T

    def gate_stage(h):
        b_c = b_t[:, h:h + 1]
        logd = jnp.where(mask, b_c + r_t[h:h + 1, :], -jnp.inf)
        m_prev = m_sc[h:h + 1, 0:1]
        m_carry = b_c + m_prev
        m_row = jnp.maximum(m_carry, jnp.max(logd, axis=1, keepdims=True))
        return dict(b_c=b_c, li_c=li_t[:, h:h + 1], m_prev=m_prev, m_row=m_row,
                    d_mat=jnp.exp(logd - m_row), w_carry=jnp.exp(m_carry - m_row),
                    q=q_ref[:, h * dk:(h + 1) * dk], k=k_ref[:, h * dk:(h + 1) * dk],
                    v=v_ref[:, h * dv:(h + 1) * dv], ct=ct_sc[h], n_row=n_sc[h:h + 1, :])

    def out_stage(h, st):
        s = st["s"]
        num = _dot(s.astype(BF16), st["v"]) + st["w_carry"] * _dot(st["q"], st["ct"].astype(BF16))
        den = (jnp.sum(s, axis=1, keepdims=True)
               + st["w_carry"] * jnp.sum(st["q"].astype(F32) * st["n_row"], axis=1, keepdims=True))
        h_out = num / jnp.maximum(jnp.abs(den), jnp.exp(-st["m_row"]))
        cols = slice(h * dv, (h + 1) * dv)
        if final:
            hn = _rms(h_out + hb_ref[:, cols].astype(F32), hg_ref[:, cols])
            o_ref[:, cols] = (hn * _sigmoid(og_ref[:, cols].astype(F32))).astype(o_ref.dtype)
        else:
            o_ref[:, cols] = h_out.astype(o_ref.dtype)

    def state_stage(h, st):
        b_c, k, v = st["b_c"], st["k"], st["v"]
        m_end = st["m_row"][last:last + 1, :]
        b_end = b_c[last:last + 1, :]
        w_state = jnp.exp(b_end - b_c + st["li_c"] - m_end)
        decay = jnp.exp(b_end + st["m_prev"] - m_end)
        ct_sc[h] = decay * st["ct"] + _dot_tn(k, (w_state * v.astype(F32)).astype(BF16))
        n_sc[h:h + 1, :] = decay * st["n_row"] + jnp.sum(w_state * k.astype(F32), axis=0, keepdims=True)
        m_sc[h:h + 1, :] = jnp.broadcast_to(m_end, (1, LANES))

    for h0 in range(0, M_HEADS, HEAD_INTERLEAVE):
        heads = range(h0, h0 + HEAD_INTERLEAVE)
        st = {h: gate_stage(h) for h in heads}
        for h in heads:
            st[h]["s"] = _dot_nt(st[h]["q"], st[h]["k"]) * st[h]["d_mat"]
        for h in heads:
            out_stage(h, st[h])
        for h in heads:
            state_stage(h, st[h])
        if side:
            _side_piece(sa_ref, sw_ref, so_ref, h0 // HEAD_INTERLEAVE)


def _mlstm_pass(qkv, gates, gate_b, *, batch, ncc, nlc, d, reverse, extra=None, side=None):
    assert M_HEADS // HEAD_INTERLEAVE == SIDE_PIECES
    hv = qkv.shape[1] // 2
    hk = hv // 2
    dk, dv = hk // M_HEADS, hv // M_HEADS
    final = extra is not None

    def comb_idx(b, t):
        cc = (ncc - 1 - t) if reverse else t
        lc = (nlc - 1 - (t - ncc)) if reverse else (t - ncc)
        return jnp.where(t < ncc, batch * nlc + b * ncc + cc, b * nlc + lc)

    def lat_idx(b, t):
        tt = jnp.maximum(t - ncc, 0)
        return b * nlc + ((nlc - 1 - tt) if reverse else tt)

    lat_spec = pl.BlockSpec((CHUNK, hv), lambda b, t: (lat_idx(b, t), 0))
    in_specs = [pl.BlockSpec((CHUNK, hk), lambda b, t: (comb_idx(b, t), 0)),
                pl.BlockSpec((CHUNK, hk), lambda b, t: (comb_idx(b, t), 1)),
                pl.BlockSpec((CHUNK, hv), lambda b, t: (comb_idx(b, t), 1)),
                pl.BlockSpec((CHUNK, LANES), lambda b, t: (comb_idx(b, t), 0)),
                pl.BlockSpec((1, LANES), lambda b, t: (0, 0))]
    args = [qkv, qkv, qkv, gates, gate_b]
    if final:
        in_specs += [lat_spec, lat_spec, pl.BlockSpec((1, hv), lambda b, t: (0, 0))]
        args += list(extra)
    out_specs = [lat_spec]
    out_shape = [jax.ShapeDtypeStruct((batch * nlc * CHUNK, hv), BF16)]
    if side:
        in_specs += side.in_specs()
        args += [side.a, side.w]
        out_specs.append(side.out_spec())
        out_shape.append(side.out_shape())
    outs = pl.pallas_call(
        functools.partial(_mlstm_kernel, reverse=reverse, d=d, final=final, side=bool(side), dk=dk, dv=dv),
        grid=(batch, ncc + nlc),
        in_specs=in_specs,
        out_specs=out_specs,
        out_shape=out_shape,
        scratch_shapes=[pltpu.VMEM((M_HEADS, dk, dv), F32),
                        pltpu.VMEM((M_HEADS, dk), F32),
                        pltpu.VMEM((M_HEADS, LANES), F32)],
        compiler_params=_params(2), name="mlstm_final" if final else "mlstm_bwd")(*args)
    return outs if side else outs[0]


def _store_col_major(o_ref, scr):
    rows, cols, _ = scr.shape
    step = BF16_ROWS
    for p in range(cols // step):
        piece = pltpu.einshape("rcd->crd", scr[:, p * step:(p + 1) * step, :])
        o_ref[0, p * step * rows:(p + 1) * step * rows, :] = piece.reshape(step * rows, piece.shape[-1])


def _conv_kernel(x_ref, xc_ref, w_ref, b_ref, o_ref, scr):
    rows, cols = x_ref.shape[1], x_ref.shape[2]
    tlen = xc_ref.shape[1]
    w = w_ref[...]
    w0, w1, w2, w3, bias = w[0:1, :], w[1:2, :], w[2:3, :], w[3:4, :], b_ref[...]
    cidx = lax.broadcasted_iota(jnp.int32, (cols, 1), 0)

    xc = xc_ref[0].astype(F32)
    tidx = lax.broadcasted_iota(jnp.int32, (tlen, 1), 0)
    prev = jnp.where(tidx >= 1, pltpu.roll(xc, 1, axis=0), 0.0)
    nxt1 = jnp.where(tidx <= tlen - 2, pltpu.roll(xc, tlen - 1, axis=0), 0.0)
    nxt2 = jnp.where(tidx <= tlen - 3, pltpu.roll(xc, tlen - 2, axis=0), 0.0)
    o_ref[0, rows * cols:rows * cols + tlen, :] = _silu(
        w0 * prev + w1 * xc + w2 * nxt1 + w3 * nxt2 + bias).astype(o_ref.dtype)

    def ld(r):
        return x_ref[0, r].astype(F32)

    def from_prev_col(a):
        return jnp.where(cidx >= 1, pltpu.roll(a, 1, axis=0), 0.0)

    def from_next_col(a):
        return jnp.where(cidx <= cols - 2, pltpu.roll(a, cols - 1, axis=0), 0.0)

    def emit(r, prev, cur, nxt1, nxt2):
        scr[r] = _silu(w0 * prev + w1 * cur + w2 * nxt1 + w3 * nxt2 + bias).astype(scr.dtype)

    emit(0, from_prev_col(ld(rows - 1)), ld(0), ld(1), ld(2))

    def body(r, carry):
        emit(r, ld(r - 1), ld(r), ld(r + 1), ld(r + 2))
        return carry

    lax.fori_loop(1, rows - 2, body, 0)
    top0 = from_next_col(ld(0))
    emit(rows - 2, ld(rows - 3), ld(rows - 2), ld(rows - 1), top0)
    emit(rows - 1, ld(rows - 2), ld(rows - 1), top0, from_next_col(ld(1)))
    _store_col_major(o_ref, scr)


def _conv(x4, x3, w, b, ch_off, ch_n, cb):
    assert ch_off % cb == 0 and ch_n % cb == 0
    off = ch_off // cb
    bsz, rows, cols, _ = x4.shape
    tlen = x3.shape[1]
    return pl.pallas_call(
        _conv_kernel,
        grid=(bsz, ch_n // cb),
        in_specs=[pl.BlockSpec((1, rows, cols, cb), lambda b_, j: (b_, 0, 0, j + off)),
                  pl.BlockSpec((1, tlen, cb), lambda b_, j: (b_, 0, j + off)),
                  pl.BlockSpec((S_CONV, cb), lambda b_, j: (0, j + off)),
                  pl.BlockSpec((1, cb), lambda b_, j: (0, j + off))],
        out_specs=pl.BlockSpec((1, rows * cols + tlen, cb), lambda b_, j: (b_, 0, j)),
        out_shape=jax.ShapeDtypeStruct((bsz, rows * cols + tlen, ch_n), BF16),
        scratch_shapes=[pltpu.VMEM((rows, cols, cb), BF16)],
        compiler_params=_params(2), name="conv")(x4, x3, w, b)


def _dt_layout_kernel(x_ref, xc_ref, o_ref):
    t_lat = x_ref.shape[1] * x_ref.shape[2]
    _store_col_major(o_ref, x_ref.at[0])
    o_ref[0, t_lat:t_lat + xc_ref.shape[1], :] = xc_ref[0]


def _dt_layout(dt_lat4, dt_ctx3):
    bsz, rows, cols, n = dt_lat4.shape
    tlen = dt_ctx3.shape[1]
    t_all = rows * cols + tlen
    return pl.pallas_call(
        _dt_layout_kernel,
        grid=(bsz,),
        in_specs=[pl.BlockSpec((1, rows, cols, n), lambda b_: (b_, 0, 0, 0)),
                  pl.BlockSpec((1, tlen, n), lambda b_: (b_, 0, 0))],
        out_specs=pl.BlockSpec((1, t_all, n), lambda b_: (b_, 0, 0)),
        out_shape=jax.ShapeDtypeStruct((bsz, t_all, n), F32),
        compiler_params=_params(1), name="dt_layout")(dt_lat4, dt_ctx3)


def _split2(x):
    hi = x.astype(BF16)
    return hi, (x - hi.astype(F32)).astype(BF16)


def _head_selector(first, hpg):
    gw = hpg * S_HEAD_DIM
    row = lax.broadcasted_iota(jnp.int32, (LANES, gw), 0)
    head = lax.broadcasted_iota(jnp.int32, (LANES, gw), 1) // S_HEAD_DIM
    return jnp.where(row == first + head, 1.0, 0.0).astype(BF16)


def _expand_heads(stack, sel, parts):
    full = _dot(stack, sel)
    out, r0 = [], 0
    for rows, n_pieces in parts:
        acc = full[r0:r0 + rows]
        for p in range(1, n_pieces):
            acc = acc + full[r0 + p * rows:r0 + (p + 1) * rows]
        out.append(acc)
        r0 += n_pieces * rows
    return out


def _ssd_kernel(*refs, reverse, col0, final, side, hpg):
    n_in = 6 + (2 if final else 0)
    scan_in, refs = refs[:n_in], refs[n_in:]
    if side:
        sa_ref, sw_ref, o_ref, so_ref, h_sc = refs
    else:
        o_ref, h_sc = refs
    x_ref, b_ref, c_ref, dt_ref, dtb_ref, alog_ref = scan_in[:6]
    if final:
        yb_ref, skip_ref = scan_in[6:]
    gw = hpg * S_HEAD_DIM

    @pl.when(pl.program_id(1) == 0)
    def _():
        h_sc[...] = jnp.zeros_like(h_sc)

    mask, tri = _time_masks(reverse)
    last = 0 if reverse else CHUNK - 1
    lane_head = lax.broadcasted_iota(jnp.int32, (CHUNK, gw), 1) // S_HEAD_DIM

    dt = _softplus(dt_ref[0] + dtb_ref[...])
    cum = _cumsum_time(tri, -dt * jnp.exp(alog_ref[...]))
    cum_t = cum.T
    dt_t = dt.T
    cum_end = cum[last:last + 1, :]
    e_end = jnp.broadcast_to(jnp.exp(cum_end), (BF16_ROWS, LANES))
    stack = jnp.concatenate([jnp.exp(cum).astype(BF16), (dt * jnp.exp(cum_end - cum)).astype(BF16),
                             *_split2(e_end)], axis=0)
    parts = ((CHUNK, 1), (CHUNK, 1), (BF16_ROWS, 2))

    for g in range(S_GROUPS):
        first = col0 + g * hpg
        cols = slice(g * gw, (g + 1) * gw)
        xg_b = x_ref[0, :, cols]
        xg = xg_b.astype(F32)
        bg = b_ref[0, :, g * S_STATE:(g + 1) * S_STATE]
        cg = c_ref[0, :, g * S_STATE:(g + 1) * S_STATE]
        cb = _dot_nt(cg, bg)
        ws = []
        for r in range(hpg):
            idx = first + r
            seg = jnp.where(mask, cum[:, idx:idx + 1] - cum_t[idx:idx + 1, :], -jnp.inf)
            ws.append((cb * jnp.exp(seg) * dt_t[idx:idx + 1, :]).astype(BF16))
        w_cat = jnp.concatenate(ws, axis=1)
        e_cum_x, e_rest_x, e_end_x = _expand_heads(stack, _head_selector(first, hpg), parts)
        bd = jnp.concatenate([jnp.where(lane_head == r, xg_b, jnp.zeros_like(xg_b))
                              for r in range(hpg)], axis=0)
        h_t = h_sc[g]
        y = _dot(w_cat, bd) + e_cum_x * _dot(cg, h_t.astype(BF16))
        wx = (xg * e_rest_x).astype(BF16)
        h_sc[g] = e_end_x[0:1] * h_t + _dot_tn(bg, wx)
        if final:
            y = y + yb_ref[0, :, cols].astype(F32) + skip_ref[:, cols] * xg
        o_ref[0, :, cols] = y.astype(o_ref.dtype)
        if side and g % (S_GROUPS // SIDE_PIECES) == S_GROUPS // SIDE_PIECES - 1:
            _side_piece(sa_ref, sw_ref, so_ref, g // (S_GROUPS // SIDE_PIECES))


def _ssd_pass(xs, bm, cm, dt, dt_bias, a_log, *, ncc, nlc, col0, reverse, hpg, extra=None, side=None):
    bsz, _, ci = xs.shape
    sbc = S_GROUPS * S_STATE
    final = extra is not None

    def any_idx(b, t):
        cc = (ncc - 1 - t) if reverse else t
        lc = (nlc - 1 - (t - ncc)) if reverse else (t - ncc)
        return (b, jnp.where(t < ncc, nlc + cc, lc), 0)

    def lat_idx(b, t):
        tt = jnp.maximum(t - ncc, 0)
        return (b, (nlc - 1 - tt) if reverse else tt, 0)

    vec = pl.BlockSpec((1, LANES), lambda b, t: (0, 0))
    in_specs = [pl.BlockSpec((1, CHUNK, ci), any_idx), pl.BlockSpec((1, CHUNK, sbc), any_idx),
                pl.BlockSpec((1, CHUNK, sbc), any_idx), pl.BlockSpec((1, CHUNK, LANES), any_idx),
                vec, vec]
    args = [xs, bm, cm, dt, dt_bias, a_log]
    if final:
        in_specs += [pl.BlockSpec((1, CHUNK, ci), lat_idx), pl.BlockSpec((1, ci), lambda b, t: (0, 0))]
        args += list(extra)
    out_specs = [pl.BlockSpec((1, CHUNK, ci), lat_idx)]
    out_shape = [jax.ShapeDtypeStruct((bsz, nlc * CHUNK, ci), BF16)]
    if side:
        in_specs += side.in_specs()
        args += [side.a, side.w]
        out_specs.append(side.out_spec())
        out_shape.append(side.out_shape())
    outs = pl.pallas_call(
        functools.partial(_ssd_kernel, reverse=reverse, col0=col0, final=final, side=bool(side), hpg=hpg),
        grid=(bsz, ncc + nlc),
        in_specs=in_specs,
        out_specs=out_specs,
        out_shape=out_shape,
        scratch_shapes=[pltpu.VMEM((S_GROUPS, S_STATE, hpg * S_HEAD_DIM), F32)],
        compiler_params=_params(2), name="ssd_final" if final else "ssd_bwd")(*args)
    return outs if side else outs[0]


def _ssd_out_kernel(y_ref, z_ref, g_ref, o_ref):
    y = pltpu.einshape("crd->rcd", y_ref[0]).astype(F32)
    yt = y * _silu(z_ref[0].astype(F32))
    o_ref[0] = _rms(yt, g_ref[...]).astype(o_ref.dtype)


def _ssd_out(y_cm, oz4, norm_g, gw):
    bsz, cols, rows, ci = y_cm.shape
    rb = BF16_ROWS
    assert rows % rb == 0 and oz4.shape[-1] == 2 * ci
    zoff = ci // gw
    return pl.pallas_call(
        _ssd_out_kernel,
        grid=(bsz, rows // rb, ci // gw),
        in_specs=[pl.BlockSpec((1, cols, rb, gw), lambda b, r, g: (b, 0, r, g)),
                  pl.BlockSpec((1, rb, cols, gw), lambda b, r, g: (b, r, 0, g + zoff)),
                  pl.BlockSpec((1, gw), lambda b, r, g: (0, g))],
        out_specs=pl.BlockSpec((1, rb, cols, gw), lambda b, r, g: (b, r, 0, g)),
        out_shape=jax.ShapeDtypeStruct((bsz, rows, cols, ci), BF16),
        compiler_params=_params(3), name="ssd_out")(y_cm, oz4, norm_g)


def _pad_cols(a, n):
    return jnp.pad(a, ((0, 0), (0, n - a.shape[1])))


def kernel(x, c, ctx, c_ctx, w_ada, b_ada, norm_g, w_in, m_gate_b, m_norm_g, s_conv_w, s_conv_b,
           s_dt_bias, s_a_log, s_d, s_norm_g, w_bm, w_bs, w_out, w_ffn_gate, w_ffn_up, w_ffn_down):
    batch, seq, d_model = x.shape
    ctx_len = ctx.shape[1]
    depth = w_in.shape[0]
    rows = seq // GRID_W
    assert depth == 1 and rows * 2 == CHUNK and batch + 1 <= MOD_ROWS
    assert ctx_len % CHUNK == 0 and seq % CHUNK == 0
    m_qk = d_model // 2
    m_v = d_model
    s_inner = d_model
    s_heads = s_inner // S_HEAD_DIM
    hpg = s_heads // S_GROUPS
    gw = hpg * S_HEAD_DIM
    s_bc = S_GROUPS * S_STATE
    conv_ch = s_inner + 2 * s_bc
    n_gates = N_DIR * 2 * M_HEADS
    assert N_DIR * s_heads <= LANES and n_gates <= LANES and hpg % 2 == 0
    ncc, nlc = ctx_len // CHUNK, seq // CHUNK
    n_ctx, n_lat = batch * ctx_len, batch * seq
    t_all = seq + ctx_len
    li = 0

    cvec = jnp.zeros((MOD_ROWS, d_model), F32).at[:batch].set(c).at[batch].set(c_ctx)
    mod = _ada(cvec, w_ada[li], b_ada[li][None, :], tn=512)
    mod3 = mod.reshape(MOD_ROWS * 6, 1, d_model)
    g_n = norm_g[li]

    u = _norm_mod(x.reshape(n_lat, d_model), ctx.reshape(n_ctx, d_model), g_n[0][None, :], mod3,
                  batch, tm=min(256, n_ctx))

    wt = w_in[li].T
    tn = 512
    src, o0 = {}, 0
    for name, n in (("q", m_qk), ("k", m_qk), ("v", m_v), ("gates", n_gates), ("xbc", conv_ch),
                    ("dt", N_DIR * s_heads), ("o", m_v), ("z", s_inner), ("gm", d_model), ("gs", d_model)):
        src[name] = (o0, n)
        o0 += n
    packed = ("q", "k", "v", "xbc", "o", "z", "gm", "gs")
    offsets, dst, p0 = [], {}, 0
    for name in packed:
        s0, n = src[name]
        assert n % tn == 0
        offsets += [s0 + i * tn for i in range(n // tn)]
        dst[name] = p0
        p0 += n
    w_pk = _wprep(wt, offsets, tn, q_blocks=m_qk // tn, q_scale=float(m_qk // M_HEADS) ** -0.5)
    w_aux = _wprep(wt, [src["gates"][0], src["dt"][0]], LANES, q_blocks=0, q_scale=1.0)

    tm_tok = min(1024, n_ctx)
    tn_big = 1024
    qkv = _matmul(u, w_pk, BF16, tm_tok, tn_big, 0, None, dst["q"], 2 * m_qk + m_v, name="proj_qkv")
    xbc_l = _matmul(u, w_pk, BF16, tm_tok, tn_big, 0, n_lat, dst["xbc"], conv_ch, name="proj_xbc_lat")
    xbc_c = _matmul(u, w_pk, BF16, tm_tok, tn_big, n_lat, n_ctx, dst["xbc"], conv_ch, name="proj_xbc_ctx")
    gates = _matmul(u, w_aux, F32, tm_tok, LANES, 0, None, 0, LANES, name="proj_gates")
    dt_l = _matmul(u, w_aux, F32, tm_tok, LANES, 0, n_lat, LANES, LANES, name="proj_dt_lat")
    dt_c = _matmul(u, w_aux, F32, tm_tok, LANES, n_lat, n_ctx, LANES, LANES, name="proj_dt_ctx")

    def side(name):
        return _Side(u, w_pk, n_lat, dst[name], 2 * d_model, min(1024, n_lat), tn_big,
                     ncc + nlc, batch * (ncc + nlc))

    gate_b = _pad_cols(m_gate_b[li].reshape(1, -1).astype(F32), LANES)
    common = dict(batch=batch, ncc=ncc, nlc=nlc)
    h_bwd, oz = _mlstm_pass(qkv, gates, gate_b, d=1, reverse=True, side=side("o"), **common)
    y_m, gmgs = _mlstm_pass(qkv, gates, gate_b, d=0, reverse=False, side=side("gm"),
                            extra=(h_bwd, oz, m_norm_g[li][None, :].astype(F32)), **common)

    cw, cbias = s_conv_w[li].astype(F32), s_conv_b[li][None, :].astype(F32)
    xl4 = xbc_l.reshape(batch, rows, GRID_W, conv_ch)
    xc3 = xbc_c.reshape(batch, ctx_len, conv_ch)
    scan_in = [_conv(xl4, xc3, cw, cbias, o_, n_, 512)
               for o_, n_ in ((0, s_inner), (s_inner, s_bc), (s_inner + s_bc, s_bc))]
    scan_in.append(_dt_layout(dt_l.reshape(batch, rows, GRID_W, LANES),
                              dt_c.reshape(batch, ctx_len, LANES)))
    dtb = _pad_cols(s_dt_bias[li].reshape(1, -1).astype(F32), LANES)
    alog = _pad_cols(s_a_log[li].reshape(1, -1).astype(F32), LANES)
    skip = jnp.repeat(s_d[li].astype(F32), S_HEAD_DIM)[None, :]
    scan = dict(ncc=ncc, nlc=nlc, hpg=hpg)
    y_bwd = _ssd_pass(*scan_in, dtb, alog, col0=s_heads, reverse=True, **scan)
    y_tot = _ssd_pass(*scan_in, dtb, alog, col0=0, reverse=False, extra=(y_bwd, skip), **scan)
    y_s = _ssd_out(y_tot.reshape(batch, GRID_W, rows, s_inner),
                   oz.reshape(batch, rows, GRID_W, 2 * s_inner),
                   s_norm_g[li][None, :].astype(F32), gw).reshape(n_lat, s_inner)

    tm_lat = min(1024, n_lat)
    mix_pre = _merge(y_m, y_s, w_bm[li].astype(BF16), w_bs[li].astype(BF16), gmgs, tm_lat, 256)
    mix = _matmul(mix_pre, w_out[li].astype(BF16), BF16, tm_lat, tn_big, name="out_proj")
    x2 = x.reshape(n_lat, d_model)
    x_new, h_mod = _resid_mid(mix, x2, g_n[1][None, :], g_n[2][None, :], mod3, batch, tm=256)

    act = _swiglu_up(h_mod, w_ffn_gate[li].astype(BF16), w_ffn_up[li].astype(BF16),
                     min(2048, n_lat), 256)
    hf = _matmul(act, w_ffn_down[li].astype(BF16), BF16, min(512, n_lat), 512, name="ffn_down")
    out = _resid_out(hf, x_new, g_n[3][None, :], mod3, batch, tm=256)
    return out.reshape(batch, seq, d_model)
```

```python
import functools

import jax
import jax.numpy as jnp
from jax import lax
from jax.experimental import pallas as pl
from jax.experimental.pallas import tpu as pltpu

F32 = jnp.float32
BF16 = jnp.bfloat16

CHUNK = 128
GRID_W = 64
EPS = 1e-6
N_DIR = 2
M_HEADS = 8
S_GROUPS = 8
S_HEAD_DIM = 64
S_STATE = 128
S_CONV = 4
LANES = 128
BF16_ROWS = 16
MOD_ROWS = 8
W_ALIGN = 32
HEAD_INTERLEAVE = 2
VMEM_LIMIT = 56 * 1024 * 1024


def _params(n_axes):
    return pltpu.CompilerParams(dimension_semantics=("arbitrary",) * n_axes,
                                vmem_limit_bytes=VMEM_LIMIT)


def _dot(a, b):
    return jnp.dot(a, b, preferred_element_type=F32)


def _dot_nt(a, b):
    return lax.dot_general(a, b, (((1,), (1,)), ((), ())), preferred_element_type=F32)


def _dot_tn(a, b):
    return lax.dot_general(a, b, (((0,), (0,)), ((), ())), preferred_element_type=F32)


def _cumsum_time(tri, x):
    hi = x.astype(BF16)
    r1 = x - hi.astype(F32)
    mid = r1.astype(BF16)
    lo = (r1 - mid.astype(F32)).astype(BF16)
    return _dot(tri, hi) + _dot(tri, mid) + _dot(tri, lo)


def _sigmoid(x):
    return 1.0 / (1.0 + jnp.exp(-x))


def _silu(x):
    return x * _sigmoid(x)


def _softplus(x):
    return jnp.maximum(x, 0.0) + jnp.log(1.0 + jnp.exp(-jnp.abs(x)))


def _time_masks(reverse):
    jj = lax.broadcasted_iota(jnp.int32, (CHUNK, CHUNK), 0)
    ss = lax.broadcasted_iota(jnp.int32, (CHUNK, CHUNK), 1)
    mask = (ss >= jj) if reverse else (ss <= jj)
    return mask, jnp.where(mask, 1.0, 0.0).astype(BF16)


def _mm_kernel(a_ref, w_ref, *rest, n_casts):
    srcs, o_ref, dsts = rest[:n_casts], rest[n_casts], rest[n_casts + 1:]
    o_ref[...] = _dot(a_ref[...].astype(BF16), w_ref[...].astype(BF16)).astype(o_ref.dtype)
    for src, dst in zip(srcs, dsts):
        dst[...] = src[...].astype(dst.dtype)


def _matmul(a, w, out_dtype, tm, tn, row_off=0, rows=None, col_off=0, cols=None, name="matmul",
            casts=()):
    k = a.shape[1]
    rows = a.shape[0] - row_off if rows is None else rows
    cols = w.shape[1] - col_off if cols is None else cols
    assert rows % tm == 0 and cols % tn == 0 and row_off % tm == 0 and col_off % tn == 0
    roff, coff = row_off // tm, col_off // tn
    ni, nj = rows // tm, cols // tn
    in_specs = [pl.BlockSpec((tm, k), lambda i, j: (i + roff, 0)),
                pl.BlockSpec((k, tn), lambda i, j: (0, j + coff))]
    out_specs = [pl.BlockSpec((tm, tn), lambda i, j: (i, j))]
    out_shape = [jax.ShapeDtypeStruct((rows, cols), out_dtype)]
    for src, (br, bc) in casts:
        assert src.shape[0] % br == 0 and src.shape[1] % bc == 0
        nbc = src.shape[1] // bc
        n_blk = (src.shape[0] // br) * nbc
        assert n_blk <= ni * nj, (n_blk, ni * nj)

        def blk(i, j, nbc=nbc, n_blk=n_blk):
            s = jnp.minimum(i * nj + j, n_blk - 1)
            return s // nbc, s % nbc

        in_specs.append(pl.BlockSpec((br, bc), blk))
        out_specs.append(pl.BlockSpec((br, bc), blk))
        out_shape.append(jax.ShapeDtypeStruct(src.shape, BF16))
    outs = pl.pallas_call(
        functools.partial(_mm_kernel, n_casts=len(casts)),
        grid=(ni, nj),
        in_specs=in_specs,
        out_specs=out_specs,
        out_shape=out_shape,
        compiler_params=_params(2), name=name)(a, w, *[src for src, _ in casts])
    return outs if casts else outs[0]


def _wprep_kernel(off_ref, w_ref, o_ref, *, q_blocks, q_scale):
    del off_ref
    scale = jnp.where(pl.program_id(0) < q_blocks, q_scale, 1.0)
    o_ref[...] = (w_ref[...] * scale).T.astype(BF16)


def _wprep(wt, offsets, tn, q_blocks, q_scale):
    k = wt.shape[1]
    nblk = len(offsets)
    assert all(o % W_ALIGN == 0 for o in offsets)
    return pl.pallas_call(
        functools.partial(_wprep_kernel, q_blocks=q_blocks, q_scale=q_scale),
        grid_spec=pltpu.PrefetchScalarGridSpec(
            num_scalar_prefetch=1, grid=(nblk,),
            in_specs=[pl.BlockSpec((pl.Element(tn), pl.Element(k)),
                                   lambda j, off: (pl.multiple_of(off[j], W_ALIGN), 0))],
            out_specs=pl.BlockSpec((k, tn), lambda j, off: (0, j))),
        out_shape=jax.ShapeDtypeStruct((k, nblk * tn), BF16),
        compiler_params=_params(1), name="w_in_prep")(jnp.asarray(offsets, jnp.int32), wt)


def _ada_kernel(c_ref, w_ref, b_ref, o_ref):
    a = _silu(c_ref[...]).astype(BF16)
    o_ref[...] = _dot(a, w_ref[...].astype(BF16)) + b_ref[...]


def _ada(cvec, w, b, tn):
    k, n = w.shape
    return pl.pallas_call(
        _ada_kernel,
        grid=(n // tn,),
        in_specs=[pl.BlockSpec((MOD_ROWS, k), lambda j: (0, 0)),
                  pl.BlockSpec((k, tn), lambda j: (0, j)),
                  pl.BlockSpec((1, tn), lambda j: (0, j))],
        out_specs=pl.BlockSpec((MOD_ROWS, tn), lambda j: (0, j)),
        out_shape=jax.ShapeDtypeStruct((MOD_ROWS, n), F32),
        compiler_params=_params(1), name="ada")(cvec, w, b)


def _merge_kernel(ym_ref, ys_ref, wm_ref, ws_ref, gm_ref, gs_ref, o_ref):
    am = _dot(ym_ref[...], wm_ref[...])
    a_s = _dot(ys_ref[...], ws_ref[...])
    o_ref[...] = (_sigmoid(gm_ref[...].astype(F32)) * am
                  + _sigmoid(gs_ref[...].astype(F32)) * a_s).astype(o_ref.dtype)


def _merge(ym, ys, wm, ws, gates, tm, tn):
    m, k = ym.shape
    n = wm.shape[1]
    assert gates.shape[1] == 2 * n
    a_spec = pl.BlockSpec((tm, k), lambda i, j: (i, 0))
    w_spec = pl.BlockSpec((k, tn), lambda i, j: (0, j))
    return pl.pallas_call(
        _merge_kernel,
        grid=(m // tm, n // tn),
        in_specs=[a_spec, a_spec, w_spec, w_spec,
                  pl.BlockSpec((tm, tn), lambda i, j: (i, j)),
                  pl.BlockSpec((tm, tn), lambda i, j: (i, j + n // tn))],
        out_specs=pl.BlockSpec((tm, tn), lambda i, j: (i, j)),
        out_shape=jax.ShapeDtypeStruct((m, n), BF16),
        compiler_params=_params(2), name="merge")(ym, ys, wm, ws, gates, gates)


def _swiglu_kernel(a_ref, wg_ref, wu_ref, o_ref):
    a = a_ref[...]
    o_ref[...] = (_silu(_dot(a, wg_ref[...])) * _dot(a, wu_ref[...])).astype(o_ref.dtype)


def _swiglu_up(a, wg, wu, tm, tn):
    m, k = a.shape
    n = wg.shape[1]
    w_spec = pl.BlockSpec((k, tn), lambda i, j: (0, j))
    return pl.pallas_call(
        _swiglu_kernel,
        grid=(m // tm, n // tn),
        in_specs=[pl.BlockSpec((tm, k), lambda i, j: (i, 0)), w_spec, w_spec],
        out_specs=pl.BlockSpec((tm, tn), lambda i, j: (i, j)),
        out_shape=jax.ShapeDtypeStruct((m, n), BF16),
        compiler_params=_params(2), name="swiglu_up")(a, wg, wu)


def _rms(x, g):
    return x * lax.rsqrt(jnp.mean(x * x, axis=-1, keepdims=True) + EPS) * g


def _norm_mod_kernel(xl_ref, xc_ref, g_ref, sh_ref, sc_ref, o_ref, *, n_lat_blocks):
    i = pl.program_id(0)

    def emit(x):
        o_ref[...] = (_rms(x, g_ref[...]) * (1.0 + sc_ref[0]) + sh_ref[0]).astype(o_ref.dtype)

    @pl.when(i < n_lat_blocks)
    def _():
        emit(xl_ref[...])

    @pl.when(i >= n_lat_blocks)
    def _():
        emit(xc_ref[...])


def _norm_mod(xl, xc, g, mod3, batch, tm):
    d = xl.shape[1]
    nlb = xl.shape[0] // tm
    ncb = xc.shape[0] // tm
    per_batch = nlb // batch

    def mod_row(i):
        return jnp.where(i < nlb, jnp.minimum(i, nlb - 1) // per_batch, batch)

    return pl.pallas_call(
        functools.partial(_norm_mod_kernel, n_lat_blocks=nlb),
        grid=(nlb + ncb,),
        in_specs=[pl.BlockSpec((tm, d), lambda i: (jnp.minimum(i, nlb - 1), 0)),
                  pl.BlockSpec((tm, d), lambda i: (jnp.maximum(i - nlb, 0), 0)),
                  pl.BlockSpec((1, d), lambda i: (0, 0)),
                  pl.BlockSpec((1, 1, d), lambda i: (mod_row(i) * 6 + 0, 0, 0)),
                  pl.BlockSpec((1, 1, d), lambda i: (mod_row(i) * 6 + 1, 0, 0))],
        out_specs=pl.BlockSpec((tm, d), lambda i: (i, 0)),
        out_shape=jax.ShapeDtypeStruct(((nlb + ncb) * tm, d), BF16),
        compiler_params=_params(1), name="norm_mod")(xl, xc, g, mod3, mod3)


def _resid_mid_kernel(mix_ref, x_ref, g1_ref, g2_ref, ga_ref, sh_ref, sc_ref, xn_ref, h_ref):
    xn = x_ref[...] + ga_ref[0] * _rms(mix_ref[...].astype(F32), g1_ref[...])
    xn_ref[...] = xn
    h_ref[...] = (_rms(xn, g2_ref[...]) * (1.0 + sc_ref[0]) + sh_ref[0]).astype(h_ref.dtype)


def _resid_mid(mix, x, g1, g2, mod3, batch, tm):
    m, d = x.shape
    per_batch = m // tm // batch
    row = pl.BlockSpec((tm, d), lambda i: (i, 0))
    vec = pl.BlockSpec((1, d), lambda i: (0, 0))

    def mod(which):
        return pl.BlockSpec((1, 1, d), lambda i: ((i // per_batch) * 6 + which, 0, 0))

    return pl.pallas_call(
        _resid_mid_kernel,
        grid=(m // tm,),
        in_specs=[row, row, vec, vec, mod(2), mod(3), mod(4)],
        out_specs=[row, row],
        out_shape=[jax.ShapeDtypeStruct((m, d), F32), jax.ShapeDtypeStruct((m, d), BF16)],
        compiler_params=_params(1), name="resid_mid")(mix, x, g1, g2, mod3, mod3, mod3)


def _resid_out_kernel(hf_ref, x_ref, g_ref, ga_ref, o_ref):
    o_ref[...] = x_ref[...] + ga_ref[0] * _rms(hf_ref[...].astype(F32), g_ref[...])


def _resid_out(hf, x, g, mod3, batch, tm):
    m, d = x.shape
    per_batch = m // tm // batch
    row = pl.BlockSpec((tm, d), lambda i: (i, 0))
    return pl.pallas_call(
        _resid_out_kernel,
        grid=(m // tm,),
        in_specs=[row, row, pl.BlockSpec((1, d), lambda i: (0, 0)),
                  pl.BlockSpec((1, 1, d), lambda i: ((i // per_batch) * 6 + 5, 0, 0))],
        out_specs=row,
        out_shape=jax.ShapeDtypeStruct((m, d), F32),
        compiler_params=_params(1), name="resid_out")(hf, x, g, mod3)


SIDE_PIECES = 8


class _Side:
    def __init__(self, a, w, rows, col_off, cols, tm, tn, steps_per_batch, n_steps):
        assert rows % tm == 0 and cols % tn == 0 and col_off % tn == 0
        self.a, self.w, self.rows, self.cols, self.tm, self.tn = a, w, rows, cols, tm, tn
        self.nj, self.coff, self.spb = cols // tn, col_off // tn, steps_per_batch
        self.n_tiles = (rows // tm) * self.nj
        assert self.n_tiles <= n_steps, (self.n_tiles, n_steps)

    def _tile(self, b, t):
        s = jnp.minimum(b * self.spb + t, self.n_tiles - 1)
        return s // self.nj, s % self.nj

    def in_specs(self):
        k = self.a.shape[1]
        return [pl.BlockSpec((self.tm, k), lambda b, t: (self._tile(b, t)[0], 0)),
                pl.BlockSpec((k, self.tn), lambda b, t: (0, self._tile(b, t)[1] + self.coff))]

    def out_spec(self):
        return pl.BlockSpec((self.tm, self.tn), lambda b, t: self._tile(b, t))

    def out_shape(self):
        return jax.ShapeDtypeStruct((self.rows, self.cols), BF16)


def _side_piece(a_ref, w_ref, o_ref, p):
    tm, tn = o_ref.shape
    strips = SIDE_PIECES // 2
    rs = slice((p // strips) * (tm // 2), (p // strips + 1) * (tm // 2))
    cs = slice((p % strips) * (tn // strips), (p % strips + 1) * (tn // strips))
    o_ref[rs, cs] = _dot(a_ref[rs, :], w_ref[:, cs]).astype(o_ref.dtype)


def _mlstm_kernel(*refs, reverse, d, final, side, dk, dv):
    n_in = 5 + (3 if final else 0)
    scan_in, refs = refs[:n_in], refs[n_in:]
    if side:
        (sa_ref, sw_ref, o_ref, so_ref), refs = refs[:4], refs[4:]
    else:
        o_ref, refs = refs[0], refs[1:]
    ct_sc, n_sc, m_sc = refs
    q_ref, k_ref, v_ref, g_ref, gb_ref = scan_in[:5]
    if final:
        hb_ref, og_ref, hg_ref = scan_in[5:]

    @pl.when(pl.program_id(1) == 0)
    def _():
        ct_sc[...] = jnp.zeros_like(ct_sc)
        n_sc[...] = jnp.zeros_like(n_sc)
        m_sc[...] = jnp.zeros_like(m_sc)

    mask, tri = _time_masks(reverse)
    last = 0 if reverse else CHUNK - 1

    g = g_ref[...] + gb_ref[...]
    log_f = jnp.minimum(g, 0.0) - jnp.log(1.0 + jnp.exp(-jnp.abs(g)))
    csum = _cumsum_time(tri, log_f)
    off_i = d * 2 * M_HEADS
    off_f = off_i + M_HEADS
    li_t = g if off_i == 0 else pltpu.roll(g, LANES - off_i, axis=1)
    b_t = pltpu.roll(csum, LANES - off_f, axis=1)
    r_t = (li_t - b_t).T

    def gate_stage(h):
        b_c = b_t[:, h:h + 1]
        logd = jnp.where(mask, b_c + r_t[h:h + 1, :], -jnp.inf)
        m_prev = m_sc[h:h + 1, 0:1]
        m_carry = b_c + m_prev
        m_row = jnp.maximum(m_carry, jnp.max(logd, axis=1, keepdims=True))
        return dict(b_c=b_c, li_c=li_t[:, h:h + 1], m_prev=m_prev, m_row=m_row,
                    d_mat=jnp.exp(logd - m_row), w_carry=jnp.exp(m_carry - m_row),
                    q=q_ref[:, h * dk:(h + 1) * dk], k=k_ref[:, h * dk:(h + 1) * dk],
                    v=v_ref[:, h * dv:(h + 1) * dv], ct=ct_sc[h], n_row=n_sc[h:h + 1, :])

    def out_stage(h, st):
        s = st["s"]
        num = _dot(s.astype(BF16), st["v"]) + st["w_carry"] * _dot(st["q"], st["ct"].astype(BF16))
        den = (jnp.sum(s, axis=1, keepdims=True)
               + st["w_carry"] * jnp.sum(st["q"].astype(F32) * st["n_row"], axis=1, keepdims=True))
        h_out = num / jnp.maximum(jnp.abs(den), jnp.exp(-st["m_row"]))
        cols = slice(h * dv, (h + 1) * dv)
        if final:
            hn = _rms(h_out + hb_ref[:, cols].astype(F32), hg_ref[:, cols])
            o_ref[:, cols] = (hn * _sigmoid(og_ref[:, cols].astype(F32))).astype(o_ref.dtype)
        else:
            o_ref[:, cols] = h_out.astype(o_ref.dtype)

    def state_stage(h, st):
        b_c, k, v = st["b_c"], st["k"], st["v"]
        m_end = st["m_row"][last:last + 1, :]
        b_end = b_c[last:last + 1, :]
        w_state = jnp.exp(b_end - b_c + st["li_c"] - m_end)
        decay = jnp.exp(b_end + st["m_prev"] - m_end)
        ct_sc[h] = decay * st["ct"] + _dot_tn(k, (w_state * v.astype(F32)).astype(BF16))
        n_sc[h:h + 1, :] = decay * st["n_row"] + jnp.sum(w_state * k.astype(F32), axis=0, keepdims=True)
        m_sc[h:h + 1, :] = jnp.broadcast_to(m_end, (1, LANES))

    for h0 in range(0, M_HEADS, HEAD_INTERLEAVE):
        heads = range(h0, h0 + HEAD_INTERLEAVE)
        st = {h: gate_stage(h) for h in heads}
        for h in heads:
            st[h]["s"] = _dot_nt(st[h]["q"], st[h]["k"]) * st[h]["d_mat"]
        for h in heads:
            out_stage(h, st[h])
        if side:
            _side_piece(sa_ref, sw_ref, so_ref, 2 * (h0 // HEAD_INTERLEAVE))
        for h in heads:
            state_stage(h, st[h])
        if side:
            _side_piece(sa_ref, sw_ref, so_ref, 2 * (h0 // HEAD_INTERLEAVE) + 1)


def _mlstm_pass(qkv, gates, gate_b, *, batch, ncc, nlc, d, reverse, extra=None, side=None):
    assert 2 * (M_HEADS // HEAD_INTERLEAVE) == SIDE_PIECES
    hv = qkv.shape[1] // 2
    hk = hv // 2
    dk, dv = hk // M_HEADS, hv // M_HEADS
    final = extra is not None

    def comb_idx(b, t):
        cc = (ncc - 1 - t) if reverse else t
        lc = (nlc - 1 - (t - ncc)) if reverse else (t - ncc)
        return jnp.where(t < ncc, batch * nlc + b * ncc + cc, b * nlc + lc)

    def lat_idx(b, t):
        tt = jnp.maximum(t - ncc, 0)
        return b * nlc + ((nlc - 1 - tt) if reverse else tt)

    lat_spec = pl.BlockSpec((CHUNK, hv), lambda b, t: (lat_idx(b, t), 0))
    in_specs = [pl.BlockSpec((CHUNK, hk), lambda b, t: (comb_idx(b, t), 0)),
                pl.BlockSpec((CHUNK, hk), lambda b, t: (comb_idx(b, t), 1)),
                pl.BlockSpec((CHUNK, hv), lambda b, t: (comb_idx(b, t), 1)),
                pl.BlockSpec((CHUNK, LANES), lambda b, t: (comb_idx(b, t), 0)),
                pl.BlockSpec((1, LANES), lambda b, t: (0, 0))]
    args = [qkv, qkv, qkv, gates, gate_b]
    if final:
        in_specs += [lat_spec, lat_spec, pl.BlockSpec((1, hv), lambda b, t: (0, 0))]
        args += list(extra)
    out_specs = [lat_spec]
    out_shape = [jax.ShapeDtypeStruct((batch * nlc * CHUNK, hv), BF16)]
    if side:
        in_specs += side.in_specs()
        args += [side.a, side.w]
        out_specs.append(side.out_spec())
        out_shape.append(side.out_shape())
    outs = pl.pallas_call(
        functools.partial(_mlstm_kernel, reverse=reverse, d=d, final=final, side=bool(side), dk=dk, dv=dv),
        grid=(batch, ncc + nlc),
        in_specs=in_specs,
        out_specs=out_specs,
        out_shape=out_shape,
        scratch_shapes=[pltpu.VMEM((M_HEADS, dk, dv), F32),
                        pltpu.VMEM((M_HEADS, dk), F32),
                        pltpu.VMEM((M_HEADS, LANES), F32)],
        compiler_params=_params(2), name="mlstm_final" if final else "mlstm_bwd")(*args)
    return outs if side else outs[0]


def _store_col_major(o_ref, scr):
    rows, cols, _ = scr.shape
    step = BF16_ROWS
    for p in range(cols // step):
        piece = pltpu.einshape("rcd->crd", scr[:, p * step:(p + 1) * step, :])
        o_ref[0, p * step * rows:(p + 1) * step * rows, :] = piece.reshape(step * rows, piece.shape[-1])


def _conv_kernel(x_ref, xc_ref, w_ref, b_ref, o_ref, scr):
    rows, cols = x_ref.shape[1], x_ref.shape[2]
    tlen = xc_ref.shape[1]
    w = w_ref[...]
    w0, w1, w2, w3, bias = w[0:1, :], w[1:2, :], w[2:3, :], w[3:4, :], b_ref[...]
    cidx = lax.broadcasted_iota(jnp.int32, (cols, 1), 0)

    xc = xc_ref[0].astype(F32)
    tidx = lax.broadcasted_iota(jnp.int32, (tlen, 1), 0)
    prev = jnp.where(tidx >= 1, pltpu.roll(xc, 1, axis=0), 0.0)
    nxt1 = jnp.where(tidx <= tlen - 2, pltpu.roll(xc, tlen - 1, axis=0), 0.0)
    nxt2 = jnp.where(tidx <= tlen - 3, pltpu.roll(xc, tlen - 2, axis=0), 0.0)
    o_ref[0, rows * cols:rows * cols + tlen, :] = _silu(
        w0 * prev + w1 * xc + w2 * nxt1 + w3 * nxt2 + bias).astype(o_ref.dtype)

    def ld(r):
        return x_ref[0, r].astype(F32)

    def from_prev_col(a):
        return jnp.where(cidx >= 1, pltpu.roll(a, 1, axis=0), 0.0)

    def from_next_col(a):
        return jnp.where(cidx <= cols - 2, pltpu.roll(a, cols - 1, axis=0), 0.0)

    def emit(r, prev, cur, nxt1, nxt2):
        scr[r] = _silu(w0 * prev + w1 * cur + w2 * nxt1 + w3 * nxt2 + bias).astype(scr.dtype)

    emit(0, from_prev_col(ld(rows - 1)), ld(0), ld(1), ld(2))

    def body(r, carry):
        emit(r, ld(r - 1), ld(r), ld(r + 1), ld(r + 2))
        return carry

    lax.fori_loop(1, rows - 2, body, 0)
    top0 = from_next_col(ld(0))
    emit(rows - 2, ld(rows - 3), ld(rows - 2), ld(rows - 1), top0)
    emit(rows - 1, ld(rows - 2), ld(rows - 1), top0, from_next_col(ld(1)))
    _store_col_major(o_ref, scr)


def _conv(x4, x3, w, b, ch_off, ch_n, cb):
    assert ch_off % cb == 0 and ch_n % cb == 0
    off = ch_off // cb
    bsz, rows, cols, _ = x4.shape
    tlen = x3.shape[1]
    return pl.pallas_call(
        _conv_kernel,
        grid=(bsz, ch_n // cb),
        in_specs=[pl.BlockSpec((1, rows, cols, cb), lambda b_, j: (b_, 0, 0, j + off)),
                  pl.BlockSpec((1, tlen, cb), lambda b_, j: (b_, 0, j + off)),
                  pl.BlockSpec((S_CONV, cb), lambda b_, j: (0, j + off)),
                  pl.BlockSpec((1, cb), lambda b_, j: (0, j + off))],
        out_specs=pl.BlockSpec((1, rows * cols + tlen, cb), lambda b_, j: (b_, 0, j)),
        out_shape=jax.ShapeDtypeStruct((bsz, rows * cols + tlen, ch_n), BF16),
        scratch_shapes=[pltpu.VMEM((rows, cols, cb), BF16)],
        compiler_params=_params(2), name="conv")(x4, x3, w, b)


def _dt_layout_kernel(x_ref, xc_ref, o_ref, *, rows, cols):
    _store_col_major(o_ref, x_ref[...].reshape(rows, cols, x_ref.shape[1]))
    o_ref[0, rows * cols:rows * cols + xc_ref.shape[0], :] = xc_ref[...]


def _dt_layout(aux, col, bsz, rows, cols, tlen):
    t_lat = rows * cols
    assert (bsz * t_lat) % tlen == 0
    ctx0 = bsz * t_lat // tlen
    return pl.pallas_call(
        functools.partial(_dt_layout_kernel, rows=rows, cols=cols),
        grid=(bsz,),
        in_specs=[pl.BlockSpec((t_lat, LANES), lambda b_: (b_, col)),
                  pl.BlockSpec((tlen, LANES), lambda b_: (ctx0 + b_, col))],
        out_specs=pl.BlockSpec((1, t_lat + tlen, LANES), lambda b_: (b_, 0, 0)),
        out_shape=jax.ShapeDtypeStruct((bsz, t_lat + tlen, LANES), F32),
        compiler_params=_params(1), name="dt_layout")(aux, aux)


def _split2(x):
    hi = x.astype(BF16)
    return hi, (x - hi.astype(F32)).astype(BF16)


def _head_selector(first, hpg):
    gw = hpg * S_HEAD_DIM
    row = lax.broadcasted_iota(jnp.int32, (LANES, gw), 0)
    head = lax.broadcasted_iota(jnp.int32, (LANES, gw), 1) // S_HEAD_DIM
    return jnp.where(row == first + head, 1.0, 0.0).astype(BF16)


def _expand_heads(stack, sel, parts):
    full = _dot(stack, sel)
    out, r0 = [], 0
    for rows, n_pieces in parts:
        acc = full[r0:r0 + rows]
        for p in range(1, n_pieces):
            acc = acc + full[r0 + p * rows:r0 + (p + 1) * rows]
        out.append(acc)
        r0 += n_pieces * rows
    return out


def _ssd_kernel(*refs, reverse, col0, final, side, hpg):
    n_in = 6 + (2 if final else 0)
    scan_in, refs = refs[:n_in], refs[n_in:]
    if side:
        sa_ref, sw_ref, o_ref, so_ref, h_sc = refs
    else:
        o_ref, h_sc = refs
    x_ref, b_ref, c_ref, dt_ref, dtb_ref, alog_ref = scan_in[:6]
    if final:
        yb_ref, skip_ref = scan_in[6:]
    gw = hpg * S_HEAD_DIM

    @pl.when(pl.program_id(1) == 0)
    def _():
        h_sc[...] = jnp.zeros_like(h_sc)

    mask, tri = _time_masks(reverse)
    last = 0 if reverse else CHUNK - 1
    lane_head = lax.broadcasted_iota(jnp.int32, (CHUNK, gw), 1) // S_HEAD_DIM

    dt = _softplus(dt_ref[0] + dtb_ref[...])
    cum = _cumsum_time(tri, -dt * jnp.exp(alog_ref[...]))
    cum_t = cum.T
    dt_t = dt.T
    cum_end = cum[last:last + 1, :]
    e_end = jnp.broadcast_to(jnp.exp(cum_end), (BF16_ROWS, LANES))
    stack = jnp.concatenate([jnp.exp(cum).astype(BF16), (dt * jnp.exp(cum_end - cum)).astype(BF16),
                             *_split2(e_end)], axis=0)
    parts = ((CHUNK, 1), (CHUNK, 1), (BF16_ROWS, 2))

    for g in range(S_GROUPS):
        first = col0 + g * hpg
        cols = slice(g * gw, (g + 1) * gw)
        xg_b = x_ref[0, :, cols]
        xg = xg_b.astype(F32)
        bg = b_ref[0, :, g * S_STATE:(g + 1) * S_STATE]
        cg = c_ref[0, :, g * S_STATE:(g + 1) * S_STATE]
        cb = _dot_nt(cg, bg)
        ws = []
        for r in range(hpg):
            idx = first + r
            seg = jnp.where(mask, cum[:, idx:idx + 1] - cum_t[idx:idx + 1, :], -jnp.inf)
            ws.append((cb * jnp.exp(seg) * dt_t[idx:idx + 1, :]).astype(BF16))
        w_cat = jnp.concatenate(ws, axis=1)
        e_cum_x, e_rest_x, e_end_x = _expand_heads(stack, _head_selector(first, hpg), parts)
        bd = jnp.concatenate([jnp.where(lane_head == r, xg_b, jnp.zeros_like(xg_b))
                              for r in range(hpg)], axis=0)
        h_t = h_sc[g]
        y = _dot(w_cat, bd) + e_cum_x * _dot(cg, h_t.astype(BF16))
        wx = (xg * e_rest_x).astype(BF16)
        h_sc[g] = e_end_x[0:1] * h_t + _dot_tn(bg, wx)
        if final:
            y = y + yb_ref[0, :, cols].astype(F32) + skip_ref[:, cols] * xg
        o_ref[0, :, cols] = y.astype(o_ref.dtype)
        if side and g % (S_GROUPS // SIDE_PIECES) == S_GROUPS // SIDE_PIECES - 1:
            _side_piece(sa_ref, sw_ref, so_ref, g // (S_GROUPS // SIDE_PIECES))


def _ssd_pass(xs, bm, cm, dt, dt_bias, a_log, *, ncc, nlc, col0, reverse, hpg, extra=None, side=None):
    bsz, _, ci = xs.shape
    sbc = S_GROUPS * S_STATE
    final = extra is not None

    def any_idx(b, t):
        cc = (ncc - 1 - t) if reverse else t
        lc = (nlc - 1 - (t - ncc)) if reverse else (t - ncc)
        return (b, jnp.where(t < ncc, nlc + cc, lc), 0)

    def lat_idx(b, t):
        tt = jnp.maximum(t - ncc, 0)
        return (b, (nlc - 1 - tt) if reverse else tt, 0)

    vec = pl.BlockSpec((1, LANES), lambda b, t: (0, 0))
    in_specs = [pl.BlockSpec((1, CHUNK, ci), any_idx), pl.BlockSpec((1, CHUNK, sbc), any_idx),
                pl.BlockSpec((1, CHUNK, sbc), any_idx), pl.BlockSpec((1, CHUNK, LANES), any_idx),
                vec, vec]
    args = [xs, bm, cm, dt, dt_bias, a_log]
    if final:
        in_specs += [pl.BlockSpec((1, CHUNK, ci), lat_idx), pl.BlockSpec((1, ci), lambda b, t: (0, 0))]
        args += list(extra)
    out_specs = [pl.BlockSpec((1, CHUNK, ci), lat_idx)]
    out_shape = [jax.ShapeDtypeStruct((bsz, nlc * CHUNK, ci), BF16)]
    if side:
        in_specs += side.in_specs()
        args += [side.a, side.w]
        out_specs.append(side.out_spec())
        out_shape.append(side.out_shape())
    outs = pl.pallas_call(
        functools.partial(_ssd_kernel, reverse=reverse, col0=col0, final=final, side=bool(side), hpg=hpg),
        grid=(bsz, ncc + nlc),
        in_specs=in_specs,
        out_specs=out_specs,
        out_shape=out_shape,
        scratch_shapes=[pltpu.VMEM((S_GROUPS, S_STATE, hpg * S_HEAD_DIM), F32)],
        compiler_params=_params(2), name="ssd_final" if final else "ssd_bwd")(*args)
    return outs if side else outs[0]


def _ssd_out_kernel(y_ref, z_ref, g_ref, o_ref):
    y = pltpu.einshape("crd->rcd", y_ref[0]).astype(F32)
    yt = y * _silu(z_ref[0].astype(F32))
    o_ref[0] = _rms(yt, g_ref[...]).astype(o_ref.dtype)


def _ssd_out(y_cm, oz4, norm_g, gw):
    bsz, cols, rows, ci = y_cm.shape
    rb = BF16_ROWS
    assert rows % rb == 0 and oz4.shape[-1] == 2 * ci
    zoff = ci // gw
    return pl.pallas_call(
        _ssd_out_kernel,
        grid=(bsz, rows // rb, ci // gw),
        in_specs=[pl.BlockSpec((1, cols, rb, gw), lambda b, r, g: (b, 0, r, g)),
                  pl.BlockSpec((1, rb, cols, gw), lambda b, r, g: (b, r, 0, g + zoff)),
                  pl.BlockSpec((1, gw), lambda b, r, g: (0, g))],
        out_specs=pl.BlockSpec((1, rb, cols, gw), lambda b, r, g: (b, r, 0, g)),
        out_shape=jax.ShapeDtypeStruct((bsz, rows, cols, ci), BF16),
        compiler_params=_params(3), name="ssd_out")(y_cm, oz4, norm_g)


def _pad_cols(a, n):
    return jnp.pad(a, ((0, 0), (0, n - a.shape[1])))


def kernel(x, c, ctx, c_ctx, w_ada, b_ada, norm_g, w_in, m_gate_b, m_norm_g, s_conv_w, s_conv_b,
           s_dt_bias, s_a_log, s_d, s_norm_g, w_bm, w_bs, w_out, w_ffn_gate, w_ffn_up, w_ffn_down):
    batch, seq, d_model = x.shape
    ctx_len = ctx.shape[1]
    depth = w_in.shape[0]
    rows = seq // GRID_W
    assert depth == 1 and rows * 2 == CHUNK and batch + 1 <= MOD_ROWS
    assert ctx_len % CHUNK == 0 and seq % CHUNK == 0
    m_qk = d_model // 2
    m_v = d_model
    s_inner = d_model
    s_heads = s_inner // S_HEAD_DIM
    hpg = s_heads // S_GROUPS
    gw = hpg * S_HEAD_DIM
    s_bc = S_GROUPS * S_STATE
    conv_ch = s_inner + 2 * s_bc
    n_gates = N_DIR * 2 * M_HEADS
    assert N_DIR * s_heads <= LANES and n_gates <= LANES and hpg % 2 == 0
    ncc, nlc = ctx_len // CHUNK, seq // CHUNK
    n_ctx, n_lat = batch * ctx_len, batch * seq
    t_all = seq + ctx_len
    li = 0

    cvec = jnp.zeros((MOD_ROWS, d_model), F32).at[:batch].set(c).at[batch].set(c_ctx)
    mod = _ada(cvec, w_ada[li], b_ada[li][None, :], tn=512)
    mod3 = mod.reshape(MOD_ROWS * 6, 1, d_model)
    g_n = norm_g[li]

    u = _norm_mod(x.reshape(n_lat, d_model), ctx.reshape(n_ctx, d_model), g_n[0][None, :], mod3,
                  batch, tm=min(256, n_ctx))

    wt = w_in[li].T
    tn = 512
    src, o0 = {}, 0
    for name, n in (("q", m_qk), ("k", m_qk), ("v", m_v), ("gates", n_gates), ("xbc", conv_ch),
                    ("dt", N_DIR * s_heads), ("o", m_v), ("z", s_inner), ("gm", d_model), ("gs", d_model)):
        src[name] = (o0, n)
        o0 += n
    packed = ("q", "k", "v", "xbc", "o", "z", "gm", "gs")
    offsets, dst, p0 = [], {}, 0
    for name in packed:
        s0, n = src[name]
        assert n % tn == 0
        offsets += [s0 + i * tn for i in range(n // tn)]
        dst[name] = p0
        p0 += n
    w_pk = _wprep(wt, offsets, tn, q_blocks=m_qk // tn, q_scale=float(m_qk // M_HEADS) ** -0.5)
    w_aux = _wprep(wt, [src["gates"][0], src["dt"][0]], LANES, q_blocks=0, q_scale=1.0)

    tm_tok = min(1024, n_ctx)
    tn_big = 1024
    up_blk = (min(2048, d_model), 256)
    qkv, w_gate_b, w_up_b = _matmul(
        u, w_pk, BF16, tm_tok, tn_big, 0, None, dst["q"], 2 * m_qk + m_v, name="proj_qkv",
        casts=((w_ffn_gate[li], up_blk), (w_ffn_up[li], up_blk)))
    xbc_l, w_down_b = _matmul(
        u, w_pk, BF16, tm_tok, tn_big, 0, n_lat, dst["xbc"], conv_ch, name="proj_xbc_lat",
        casts=((w_ffn_down[li], (256, d_model)),))
    xbc_c = _matmul(u, w_pk, BF16, tm_tok, tn_big, n_lat, n_ctx, dst["xbc"], conv_ch, name="proj_xbc_ctx")
    aux = _matmul(u, w_aux, F32, tm_tok, 2 * LANES, name="proj_aux")

    def side(name):
        return _Side(u, w_pk, n_lat, dst[name], 2 * d_model, min(1024, n_lat), tn_big,
                     ncc + nlc, batch * (ncc + nlc))

    gate_b = _pad_cols(m_gate_b[li].reshape(1, -1).astype(F32), LANES)
    common = dict(batch=batch, ncc=ncc, nlc=nlc)
    h_bwd, oz = _mlstm_pass(qkv, aux, gate_b, d=1, reverse=True, side=side("o"), **common)
    y_m, gmgs = _mlstm_pass(qkv, aux, gate_b, d=0, reverse=False, side=side("gm"),
                            extra=(h_bwd, oz, m_norm_g[li][None, :].astype(F32)), **common)

    cw, cbias = s_conv_w[li].astype(F32), s_conv_b[li][None, :].astype(F32)
    xl4 = xbc_l.reshape(batch, rows, GRID_W, conv_ch)
    xc3 = xbc_c.reshape(batch, ctx_len, conv_ch)
    scan_in = [_conv(xl4, xc3, cw, cbias, o_, n_, 512)
               for o_, n_ in ((0, s_inner), (s_inner, s_bc), (s_inner + s_bc, s_bc))]
    scan_in.append(_dt_layout(aux, 1, batch, rows, GRID_W, ctx_len))
    dtb = _pad_cols(s_dt_bias[li].reshape(1, -1).astype(F32), LANES)
    alog = _pad_cols(s_a_log[li].reshape(1, -1).astype(F32), LANES)
    skip = jnp.repeat(s_d[li].astype(F32), S_HEAD_DIM)[None, :]
    scan = dict(ncc=ncc, nlc=nlc, hpg=hpg)
    y_bwd = _ssd_pass(*scan_in, dtb, alog, col0=s_heads, reverse=True, **scan)
    y_tot = _ssd_pass(*scan_in, dtb, alog, col0=0, reverse=False, extra=(y_bwd, skip), **scan)
    y_s = _ssd_out(y_tot.reshape(batch, GRID_W, rows, s_inner),
                   oz.reshape(batch, rows, GRID_W, 2 * s_inner),
                   s_norm_g[li][None, :].astype(F32), gw).reshape(n_lat, s_inner)

    tm_lat = min(1024, n_lat)
    mix_pre = _merge(y_m, y_s, w_bm[li].astype(BF16), w_bs[li].astype(BF16), gmgs, tm_lat, 256)
    mix = _matmul(mix_pre, w_out[li].astype(BF16), BF16, tm_lat, tn_big, name="out_proj")
    x2 = x.reshape(n_lat, d_model)
    x_new, h_mod = _resid_mid(mix, x2, g_n[1][None, :], g_n[2][None, :], mod3, batch, tm=256)

    act = _swiglu_up(h_mod, w_gate_b, w_up_b, min(2048, n_lat), 256)
    hf = _matmul(act, w_down_b, BF16, min(512, n_lat), 512, name="ffn_down")
    out = _resid_out(hf, x_new, g_n[3][None, :], mod3, batch, tm=256)
    return out.reshape(batch, seq, d_model)
```

```python
import functools

import jax
import jax.numpy as jnp
from jax import lax
from jax.experimental import pallas as pl
from jax.experimental.pallas import tpu as pltpu

F32 = jnp.float32
BF16 = jnp.bfloat16

CHUNK = 128
GRID_W = 64
EPS = 1e-6
N_DIR = 2
M_HEADS = 8
S_GROUPS = 8
S_HEAD_DIM = 64
S_STATE = 128
S_CONV = 4
LANES = 128
BF16_ROWS = 16
MOD_ROWS = 8
W_ALIGN = 32
HEAD_INTERLEAVE = 2
VMEM_LIMIT = 56 * 1024 * 1024


def _params(n_axes):
    return pltpu.CompilerParams(dimension_semantics=("arbitrary",) * n_axes,
                                vmem_limit_bytes=VMEM_LIMIT)


def _dot(a, b):
    return jnp.dot(a, b, preferred_element_type=F32)


def _dot_nt(a, b):
    return lax.dot_general(a, b, (((1,), (1,)), ((), ())), preferred_element_type=F32)


def _dot_tn(a, b):
    return lax.dot_general(a, b, (((0,), (0,)), ((), ())), preferred_element_type=F32)


def _cumsum_time(tri, x):
    hi = x.astype(BF16)
    r1 = x - hi.astype(F32)
    mid = r1.astype(BF16)
    lo = (r1 - mid.astype(F32)).astype(BF16)
    return _dot(tri, hi) + _dot(tri, mid) + _dot(tri, lo)


def _sigmoid(x):
    return 1.0 / (1.0 + jnp.exp(-x))


def _silu(x):
    return x * _sigmoid(x)


def _softplus(x):
    return jnp.maximum(x, 0.0) + jnp.log(1.0 + jnp.exp(-jnp.abs(x)))


def _time_masks(reverse):
    jj = lax.broadcasted_iota(jnp.int32, (CHUNK, CHUNK), 0)
    ss = lax.broadcasted_iota(jnp.int32, (CHUNK, CHUNK), 1)
    mask = (ss >= jj) if reverse else (ss <= jj)
    return mask, jnp.where(mask, 1.0, 0.0).astype(BF16)


def _mm_kernel(a_ref, w_ref, *rest, n_casts):
    srcs, o_ref, dsts = rest[:n_casts], rest[n_casts], rest[n_casts + 1:]
    o_ref[...] = _dot(a_ref[...].astype(BF16), w_ref[...].astype(BF16)).astype(o_ref.dtype)
    for src, dst in zip(srcs, dsts):
        dst[...] = src[...].astype(dst.dtype)


def _matmul(a, w, out_dtype, tm, tn, row_off=0, rows=None, col_off=0, cols=None, name="matmul",
            casts=()):
    k = a.shape[1]
    rows = a.shape[0] - row_off if rows is None else rows
    cols = w.shape[1] - col_off if cols is None else cols
    assert rows % tm == 0 and cols % tn == 0 and row_off % tm == 0 and col_off % tn == 0
    roff, coff = row_off // tm, col_off // tn
    ni, nj = rows // tm, cols // tn
    in_specs = [pl.BlockSpec((tm, k), lambda i, j: (i + roff, 0)),
                pl.BlockSpec((k, tn), lambda i, j: (0, j + coff))]
    out_specs = [pl.BlockSpec((tm, tn), lambda i, j: (i, j))]
    out_shape = [jax.ShapeDtypeStruct((rows, cols), out_dtype)]
    for src, (br, bc) in casts:
        assert src.shape[0] % br == 0 and src.shape[1] % bc == 0
        nbc = src.shape[1] // bc
        n_blk = (src.shape[0] // br) * nbc
        assert n_blk <= ni * nj, (n_blk, ni * nj)

        def blk(i, j, nbc=nbc, n_blk=n_blk):
            s = jnp.minimum(i * nj + j, n_blk - 1)
            return s // nbc, s % nbc

        in_specs.append(pl.BlockSpec((br, bc), blk))
        out_specs.append(pl.BlockSpec((br, bc), blk))
        out_shape.append(jax.ShapeDtypeStruct(src.shape, BF16))
    outs = pl.pallas_call(
        functools.partial(_mm_kernel, n_casts=len(casts)),
        grid=(ni, nj),
        in_specs=in_specs,
        out_specs=out_specs,
        out_shape=out_shape,
        compiler_params=_params(2), name=name)(a, w, *[src for src, _ in casts])
    return outs if casts else outs[0]


def _wprep_kernel(off_ref, w_ref, o_ref, *, q_blocks, q_scale):
    del off_ref
    scale = jnp.where(pl.program_id(0) < q_blocks, q_scale, 1.0)
    o_ref[...] = (w_ref[...] * scale).T.astype(BF16)


def _wprep(wt, offsets, tn, q_blocks, q_scale):
    k = wt.shape[1]
    nblk = len(offsets)
    assert all(o % W_ALIGN == 0 for o in offsets)
    return pl.pallas_call(
        functools.partial(_wprep_kernel, q_blocks=q_blocks, q_scale=q_scale),
        grid_spec=pltpu.PrefetchScalarGridSpec(
            num_scalar_prefetch=1, grid=(nblk,),
            in_specs=[pl.BlockSpec((pl.Element(tn), pl.Element(k)),
                                   lambda j, off: (pl.multiple_of(off[j], W_ALIGN), 0))],
            out_specs=pl.BlockSpec((k, tn), lambda j, off: (0, j))),
        out_shape=jax.ShapeDtypeStruct((k, nblk * tn), BF16),
        compiler_params=_params(1), name="w_in_prep")(jnp.asarray(offsets, jnp.int32), wt)


def _ada_kernel(c_ref, w_ref, b_ref, o_ref):
    a = _silu(c_ref[...]).astype(BF16)
    o_ref[...] = _dot(a, w_ref[...].astype(BF16)) + b_ref[...]


def _ada(cvec, w, b, tn):
    k, n = w.shape
    return pl.pallas_call(
        _ada_kernel,
        grid=(n // tn,),
        in_specs=[pl.BlockSpec((MOD_ROWS, k), lambda j: (0, 0)),
                  pl.BlockSpec((k, tn), lambda j: (0, j)),
                  pl.BlockSpec((1, tn), lambda j: (0, j))],
        out_specs=pl.BlockSpec((MOD_ROWS, tn), lambda j: (0, j)),
        out_shape=jax.ShapeDtypeStruct((MOD_ROWS, n), F32),
        compiler_params=_params(1), name="ada")(cvec, w, b)


def _merge_kernel(ym_ref, ys_ref, wm_ref, ws_ref, gm_ref, gs_ref, o_ref):
    am = _dot(ym_ref[...], wm_ref[...])
    a_s = _dot(ys_ref[...], ws_ref[...])
    o_ref[...] = (gm_ref[...].astype(F32) * am + gs_ref[...].astype(F32) * a_s).astype(o_ref.dtype)


def _merge(ym, ys, wm, ws, gates, tm, tn):
    m, k = ym.shape
    n = wm.shape[1]
    assert gates.shape[1] == 2 * n
    a_spec = pl.BlockSpec((tm, k), lambda i, j: (i, 0))
    w_spec = pl.BlockSpec((k, tn), lambda i, j: (0, j))
    return pl.pallas_call(
        _merge_kernel,
        grid=(m // tm, n // tn),
        in_specs=[a_spec, a_spec, w_spec, w_spec,
                  pl.BlockSpec((tm, tn), lambda i, j: (i, j)),
                  pl.BlockSpec((tm, tn), lambda i, j: (i, j + n // tn))],
        out_specs=pl.BlockSpec((tm, tn), lambda i, j: (i, j)),
        out_shape=jax.ShapeDtypeStruct((m, n), BF16),
        compiler_params=_params(2), name="merge")(ym, ys, wm, ws, gates, gates)


def _swiglu_kernel(a_ref, wg_ref, wu_ref, o_ref):
    a = a_ref[...]
    o_ref[...] = (_silu(_dot(a, wg_ref[...])) * _dot(a, wu_ref[...])).astype(o_ref.dtype)


def _swiglu_up(a, wg, wu, tm, tn):
    m, k = a.shape
    n = wg.shape[1]
    w_spec = pl.BlockSpec((k, tn), lambda i, j: (0, j))
    return pl.pallas_call(
        _swiglu_kernel,
        grid=(m // tm, n // tn),
        in_specs=[pl.BlockSpec((tm, k), lambda i, j: (i, 0)), w_spec, w_spec],
        out_specs=pl.BlockSpec((tm, tn), lambda i, j: (i, j)),
        out_shape=jax.ShapeDtypeStruct((m, n), BF16),
        compiler_params=_params(2), name="swiglu_up")(a, wg, wu)


def _rms(x, g):
    return x * lax.rsqrt(jnp.mean(x * x, axis=-1, keepdims=True) + EPS) * g


def _norm_mod_kernel(xl_ref, xc_ref, g_ref, sh_ref, sc_ref, o_ref, *, n_lat_blocks):
    i = pl.program_id(0)

    def emit(x):
        o_ref[...] = (_rms(x, g_ref[...]) * (1.0 + sc_ref[0]) + sh_ref[0]).astype(o_ref.dtype)

    @pl.when(i < n_lat_blocks)
    def _():
        emit(xl_ref[...])

    @pl.when(i >= n_lat_blocks)
    def _():
        emit(xc_ref[...])


def _norm_mod(xl, xc, g, mod3, batch, tm):
    d = xl.shape[1]
    nlb = xl.shape[0] // tm
    ncb = xc.shape[0] // tm
    per_batch = nlb // batch

    def mod_row(i):
        return jnp.where(i < nlb, jnp.minimum(i, nlb - 1) // per_batch, batch)

    return pl.pallas_call(
        functools.partial(_norm_mod_kernel, n_lat_blocks=nlb),
        grid=(nlb + ncb,),
        in_specs=[pl.BlockSpec((tm, d), lambda i: (jnp.minimum(i, nlb - 1), 0)),
                  pl.BlockSpec((tm, d), lambda i: (jnp.maximum(i - nlb, 0), 0)),
                  pl.BlockSpec((1, d), lambda i: (0, 0)),
                  pl.BlockSpec((1, 1, d), lambda i: (mod_row(i) * 6 + 0, 0, 0)),
                  pl.BlockSpec((1, 1, d), lambda i: (mod_row(i) * 6 + 1, 0, 0))],
        out_specs=pl.BlockSpec((tm, d), lambda i: (i, 0)),
        out_shape=jax.ShapeDtypeStruct(((nlb + ncb) * tm, d), BF16),
        compiler_params=_params(1), name="norm_mod")(xl, xc, g, mod3, mod3)


def _resid_mid_kernel(mix_ref, x_ref, g1_ref, g2_ref, ga_ref, sh_ref, sc_ref, xn_ref, h_ref):
    xn = x_ref[...] + ga_ref[0] * _rms(mix_ref[...].astype(F32), g1_ref[...])
    xn_ref[...] = xn
    h_ref[...] = (_rms(xn, g2_ref[...]) * (1.0 + sc_ref[0]) + sh_ref[0]).astype(h_ref.dtype)


def _resid_mid(mix, x, g1, g2, mod3, batch, tm):
    m, d = x.shape
    per_batch = m // tm // batch
    row = pl.BlockSpec((tm, d), lambda i: (i, 0))
    vec = pl.BlockSpec((1, d), lambda i: (0, 0))

    def mod(which):
        return pl.BlockSpec((1, 1, d), lambda i: ((i // per_batch) * 6 + which, 0, 0))

    return pl.pallas_call(
        _resid_mid_kernel,
        grid=(m // tm,),
        in_specs=[row, row, vec, vec, mod(2), mod(3), mod(4)],
        out_specs=[row, row],
        out_shape=[jax.ShapeDtypeStruct((m, d), F32), jax.ShapeDtypeStruct((m, d), BF16)],
        compiler_params=_params(1), name="resid_mid")(mix, x, g1, g2, mod3, mod3, mod3)


def _resid_out_kernel(hf_ref, x_ref, g_ref, ga_ref, o_ref):
    o_ref[...] = x_ref[...] + ga_ref[0] * _rms(hf_ref[...].astype(F32), g_ref[...])


def _resid_out(hf, x, g, mod3, batch, tm):
    m, d = x.shape
    per_batch = m // tm // batch
    row = pl.BlockSpec((tm, d), lambda i: (i, 0))
    return pl.pallas_call(
        _resid_out_kernel,
        grid=(m // tm,),
        in_specs=[row, row, pl.BlockSpec((1, d), lambda i: (0, 0)),
                  pl.BlockSpec((1, 1, d), lambda i: ((i // per_batch) * 6 + 5, 0, 0))],
        out_specs=row,
        out_shape=jax.ShapeDtypeStruct((m, d), F32),
        compiler_params=_params(1), name="resid_out")(hf, x, g, mod3)


SIDE_PIECES = 8


class _Side:
    def __init__(self, a, w, rows, col_off, cols, tm, tn, steps_per_batch, n_steps):
        assert rows % tm == 0 and cols % tn == 0 and col_off % tn == 0
        self.a, self.w, self.rows, self.cols, self.tm, self.tn = a, w, rows, cols, tm, tn
        self.nj, self.coff, self.spb = cols // tn, col_off // tn, steps_per_batch
        self.n_tiles = (rows // tm) * self.nj
        assert self.n_tiles <= n_steps, (self.n_tiles, n_steps)

    def _tile(self, b, t):
        s = jnp.minimum(b * self.spb + t, self.n_tiles - 1)
        return s // self.nj, s % self.nj

    def in_specs(self):
        k = self.a.shape[1]
        return [pl.BlockSpec((self.tm, k), lambda b, t: (self._tile(b, t)[0], 0)),
                pl.BlockSpec((k, self.tn), lambda b, t: (0, self._tile(b, t)[1] + self.coff))]

    def out_spec(self):
        return pl.BlockSpec((self.tm, self.tn), lambda b, t: self._tile(b, t))

    def out_shape(self):
        return jax.ShapeDtypeStruct((self.rows, self.cols), BF16)


def _side_piece(a_ref, w_ref, o_ref, p, gate):
    tm, tn = o_ref.shape
    strips = SIDE_PIECES // 2
    rs = slice((p // strips) * (tm // 2), (p // strips + 1) * (tm // 2))
    cs = slice((p % strips) * (tn // strips), (p % strips + 1) * (tn // strips))
    val = _dot(a_ref[rs, :], w_ref[:, cs])
    o_ref[rs, cs] = (_sigmoid(val) if gate else val).astype(o_ref.dtype)


def _mlstm_kernel(*refs, reverse, d, final, side, dk, dv):
    n_in = 5 + (3 if final else 0)
    scan_in, refs = refs[:n_in], refs[n_in:]
    if side:
        (sa_ref, sw_ref, o_ref, so_ref), refs = refs[:4], refs[4:]
    else:
        o_ref, refs = refs[0], refs[1:]
    ct_sc, n_sc, m_sc = refs
    q_ref, k_ref, v_ref, g_ref, gb_ref = scan_in[:5]
    if final:
        hb_ref, og_ref, hg_ref = scan_in[5:]

    @pl.when(pl.program_id(1) == 0)
    def _():
        ct_sc[...] = jnp.zeros_like(ct_sc)
        n_sc[...] = jnp.zeros_like(n_sc)
        m_sc[...] = jnp.zeros_like(m_sc)

    mask, tri = _time_masks(reverse)
    last = 0 if reverse else CHUNK - 1

    g = g_ref[...] + gb_ref[...]
    log_f = jnp.minimum(g, 0.0) - jnp.log(1.0 + jnp.exp(-jnp.abs(g)))
    csum = _cumsum_time(tri, log_f)
    off_i = d * 2 * M_HEADS
    off_f = off_i + M_HEADS
    li_t = g if off_i == 0 else pltpu.roll(g, LANES - off_i, axis=1)
    b_t = pltpu.roll(csum, LANES - off_f, axis=1)
    r_t = (li_t - b_t).T

    def gate_stage(h):
        b_c = b_t[:, h:h + 1]
        logd = jnp.where(mask, b_c + r_t[h:h + 1, :], -jnp.inf)
        m_prev = m_sc[h:h + 1, 0:1]
        m_carry = b_c + m_prev
        m_row = jnp.maximum(m_carry, jnp.max(logd, axis=1, keepdims=True))
        return dict(b_c=b_c, li_c=li_t[:, h:h + 1], m_prev=m_prev, m_row=m_row,
                    d_mat=jnp.exp(logd - m_row), w_carry=jnp.exp(m_carry - m_row),
                    q=q_ref[:, h * dk:(h + 1) * dk], k=k_ref[:, h * dk:(h + 1) * dk],
                    v=v_ref[:, h * dv:(h + 1) * dv], ct=ct_sc[h], n_row=n_sc[h:h + 1, :])

    def out_stage(h, st):
        s = st["s"]
        num = _dot(s.astype(BF16), st["v"]) + st["w_carry"] * _dot(st["q"], st["ct"].astype(BF16))
        den = (jnp.sum(s, axis=1, keepdims=True)
               + st["w_carry"] * jnp.sum(st["q"].astype(F32) * st["n_row"], axis=1, keepdims=True))
        h_out = num / jnp.maximum(jnp.abs(den), jnp.exp(-st["m_row"]))
        cols = slice(h * dv, (h + 1) * dv)
        if final:
            hn = _rms(h_out + hb_ref[:, cols].astype(F32), hg_ref[:, cols])
            o_ref[:, cols] = (hn * _sigmoid(og_ref[:, cols].astype(F32))).astype(o_ref.dtype)
        else:
            o_ref[:, cols] = h_out.astype(o_ref.dtype)

    def state_stage(h, st):
        b_c, k, v = st["b_c"], st["k"], st["v"]
        m_end = st["m_row"][last:last + 1, :]
        b_end = b_c[last:last + 1, :]
        w_state = jnp.exp(b_end - b_c + st["li_c"] - m_end)
        decay = jnp.exp(b_end + st["m_prev"] - m_end)
        ct_sc[h] = decay * st["ct"] + _dot_tn(k, (w_state * v.astype(F32)).astype(BF16))
        n_sc[h:h + 1, :] = decay * st["n_row"] + jnp.sum(w_state * k.astype(F32), axis=0, keepdims=True)
        m_sc[h:h + 1, :] = jnp.broadcast_to(m_end, (1, LANES))

    for h0 in range(0, M_HEADS, HEAD_INTERLEAVE):
        heads = range(h0, h0 + HEAD_INTERLEAVE)
        st = {h: gate_stage(h) for h in heads}
        for h in heads:
            st[h]["s"] = _dot_nt(st[h]["q"], st[h]["k"]) * st[h]["d_mat"]
        for h in heads:
            out_stage(h, st[h])
        if side:
            _side_piece(sa_ref, sw_ref, so_ref, 2 * (h0 // HEAD_INTERLEAVE), side == "gate")
        for h in heads:
            state_stage(h, st[h])
        if side:
            _side_piece(sa_ref, sw_ref, so_ref, 2 * (h0 // HEAD_INTERLEAVE) + 1, side == "gate")


def _mlstm_pass(qkv, gates, gate_b, *, batch, ncc, nlc, d, reverse, extra=None, side=None,
                side_gate=False):
    assert 2 * (M_HEADS // HEAD_INTERLEAVE) == SIDE_PIECES
    hv = qkv.shape[1] // 2
    hk = hv // 2
    dk, dv = hk // M_HEADS, hv // M_HEADS
    final = extra is not None

    def comb_idx(b, t):
        cc = (ncc - 1 - t) if reverse else t
        lc = (nlc - 1 - (t - ncc)) if reverse else (t - ncc)
        return jnp.where(t < ncc, batch * nlc + b * ncc + cc, b * nlc + lc)

    def lat_idx(b, t):
        tt = jnp.maximum(t - ncc, 0)
        return b * nlc + ((nlc - 1 - tt) if reverse else tt)

    lat_spec = pl.BlockSpec((CHUNK, hv), lambda b, t: (lat_idx(b, t), 0))
    in_specs = [pl.BlockSpec((CHUNK, hk), lambda b, t: (comb_idx(b, t), 0)),
                pl.BlockSpec((CHUNK, hk), lambda b, t: (comb_idx(b, t), 1)),
                pl.BlockSpec((CHUNK, hv), lambda b, t: (comb_idx(b, t), 1)),
                pl.BlockSpec((CHUNK, LANES), lambda b, t: (comb_idx(b, t), 0)),
                pl.BlockSpec((1, LANES), lambda b, t: (0, 0))]
    args = [qkv, qkv, qkv, gates, gate_b]
    if final:
        in_specs += [lat_spec, lat_spec, pl.BlockSpec((1, hv), lambda b, t: (0, 0))]
        args += list(extra)
    out_specs = [lat_spec]
    out_shape = [jax.ShapeDtypeStruct((batch * nlc * CHUNK, hv), BF16)]
    if side:
        in_specs += side.in_specs()
        args += [side.a, side.w]
        out_specs.append(side.out_spec())
        out_shape.append(side.out_shape())
    outs = pl.pallas_call(
        functools.partial(_mlstm_kernel, reverse=reverse, d=d, final=final, dk=dk, dv=dv,
                          side=("gate" if side_gate else "plain") if side else None),
        grid=(batch, ncc + nlc),
        in_specs=in_specs,
        out_specs=out_specs,
        out_shape=out_shape,
        scratch_shapes=[pltpu.VMEM((M_HEADS, dk, dv), F32),
                        pltpu.VMEM((M_HEADS, dk), F32),
                        pltpu.VMEM((M_HEADS, LANES), F32)],
        compiler_params=_params(2), name="mlstm_final" if final else "mlstm_bwd")(*args)
    return outs if side else outs[0]


def _store_col_major(o_ref, scr):
    rows, cols, _ = scr.shape
    step = BF16_ROWS
    for p in range(cols // step):
        piece = pltpu.einshape("rcd->crd", scr[:, p * step:(p + 1) * step, :])
        o_ref[0, p * step * rows:(p + 1) * step * rows, :] = piece.reshape(step * rows, piece.shape[-1])


def _conv_kernel(x_ref, xc_ref, w_ref, b_ref, o_ref, scr):
    rows, cols = x_ref.shape[1], x_ref.shape[2]
    tlen = xc_ref.shape[1]
    w = w_ref[...]
    w0, w1, w2, w3, bias = w[0:1, :], w[1:2, :], w[2:3, :], w[3:4, :], b_ref[...]
    cidx = lax.broadcasted_iota(jnp.int32, (cols, 1), 0)

    xc = xc_ref[0].astype(F32)
    tidx = lax.broadcasted_iota(jnp.int32, (tlen, 1), 0)
    prev = jnp.where(tidx >= 1, pltpu.roll(xc, 1, axis=0), 0.0)
    nxt1 = jnp.where(tidx <= tlen - 2, pltpu.roll(xc, tlen - 1, axis=0), 0.0)
    nxt2 = jnp.where(tidx <= tlen - 3, pltpu.roll(xc, tlen - 2, axis=0), 0.0)
    o_ref[0, rows * cols:rows * cols + tlen, :] = _silu(
        w0 * prev + w1 * xc + w2 * nxt1 + w3 * nxt2 + bias).astype(o_ref.dtype)

    def ld(r):
        return x_ref[0, r].astype(F32)

    def from_prev_col(a):
        return jnp.where(cidx >= 1, pltpu.roll(a, 1, axis=0), 0.0)

    def from_next_col(a):
        return jnp.where(cidx <= cols - 2, pltpu.roll(a, cols - 1, axis=0), 0.0)

    def emit(r, prev, cur, nxt1, nxt2):
        scr[r] = _silu(w0 * prev + w1 * cur + w2 * nxt1 + w3 * nxt2 + bias).astype(scr.dtype)

    emit(0, from_prev_col(ld(rows - 1)), ld(0), ld(1), ld(2))

    def body(r, carry):
        emit(r, ld(r - 1), ld(r), ld(r + 1), ld(r + 2))
        return carry

    lax.fori_loop(1, rows - 2, body, 0)
    top0 = from_next_col(ld(0))
    emit(rows - 2, ld(rows - 3), ld(rows - 2), ld(rows - 1), top0)
    emit(rows - 1, ld(rows - 2), ld(rows - 1), top0, from_next_col(ld(1)))
    _store_col_major(o_ref, scr)


def _conv(x4, x3, w, b, ch_off, ch_n, cb):
    assert ch_off % cb == 0 and ch_n % cb == 0
    off = ch_off // cb
    bsz, rows, cols, _ = x4.shape
    tlen = x3.shape[1]
    return pl.pallas_call(
        _conv_kernel,
        grid=(bsz, ch_n // cb),
        in_specs=[pl.BlockSpec((1, rows, cols, cb), lambda b_, j: (b_, 0, 0, j + off)),
                  pl.BlockSpec((1, tlen, cb), lambda b_, j: (b_, 0, j + off)),
                  pl.BlockSpec((S_CONV, cb), lambda b_, j: (0, j + off)),
                  pl.BlockSpec((1, cb), lambda b_, j: (0, j + off))],
        out_specs=pl.BlockSpec((1, rows * cols + tlen, cb), lambda b_, j: (b_, 0, j)),
        out_shape=jax.ShapeDtypeStruct((bsz, rows * cols + tlen, ch_n), BF16),
        scratch_shapes=[pltpu.VMEM((rows, cols, cb), BF16)],
        compiler_params=_params(2), name="conv")(x4, x3, w, b)


def _dt_layout_kernel(x_ref, xc_ref, o_ref, *, rows, cols):
    _store_col_major(o_ref, x_ref[...].reshape(rows, cols, x_ref.shape[1]))
    o_ref[0, rows * cols:rows * cols + xc_ref.shape[0], :] = xc_ref[...]


def _dt_layout(aux, col, bsz, rows, cols, tlen):
    t_lat = rows * cols
    assert (bsz * t_lat) % tlen == 0
    ctx0 = bsz * t_lat // tlen
    return pl.pallas_call(
        functools.partial(_dt_layout_kernel, rows=rows, cols=cols),
        grid=(bsz,),
        in_specs=[pl.BlockSpec((t_lat, LANES), lambda b_: (b_, col)),
                  pl.BlockSpec((tlen, LANES), lambda b_: (ctx0 + b_, col))],
        out_specs=pl.BlockSpec((1, t_lat + tlen, LANES), lambda b_: (b_, 0, 0)),
        out_shape=jax.ShapeDtypeStruct((bsz, t_lat + tlen, LANES), F32),
        compiler_params=_params(1), name="dt_layout")(aux, aux)


def _split2(x):
    hi = x.astype(BF16)
    return hi, (x - hi.astype(F32)).astype(BF16)


def _head_selector(first, hpg):
    gw = hpg * S_HEAD_DIM
    row = lax.broadcasted_iota(jnp.int32, (LANES, gw), 0)
    head = lax.broadcasted_iota(jnp.int32, (LANES, gw), 1) // S_HEAD_DIM
    return jnp.where(row == first + head, 1.0, 0.0).astype(BF16)


def _expand_heads(stack, sel, parts):
    full = _dot(stack, sel)
    out, r0 = [], 0
    for rows, n_pieces in parts:
        acc = full[r0:r0 + rows]
        for p in range(1, n_pieces):
            acc = acc + full[r0 + p * rows:r0 + (p + 1) * rows]
        out.append(acc)
        r0 += n_pieces * rows
    return out


def _ssd_kernel(*refs, reverse, col0, final, n_casts, hpg):
    n_in = 6 + (2 if final else 0)
    scan_in, refs = refs[:n_in], refs[n_in:]
    cast_src, o_ref, cast_dst, h_sc = (refs[:n_casts], refs[n_casts],
                                       refs[n_casts + 1:2 * n_casts + 1], refs[2 * n_casts + 1])
    for src, dst in zip(cast_src, cast_dst):
        dst[...] = src[...].astype(dst.dtype)
    x_ref, b_ref, c_ref, dt_ref, dtb_ref, alog_ref = scan_in[:6]
    if final:
        yb_ref, skip_ref = scan_in[6:]
    gw = hpg * S_HEAD_DIM

    @pl.when(pl.program_id(1) == 0)
    def _():
        h_sc[...] = jnp.zeros_like(h_sc)

    mask, tri = _time_masks(reverse)
    last = 0 if reverse else CHUNK - 1
    lane_head = lax.broadcasted_iota(jnp.int32, (CHUNK, gw), 1) // S_HEAD_DIM

    dt = _softplus(dt_ref[0] + dtb_ref[...])
    cum = _cumsum_time(tri, -dt * jnp.exp(alog_ref[...]))
    cum_t = cum.T
    dt_t = dt.T
    cum_end = cum[last:last + 1, :]
    e_end = jnp.broadcast_to(jnp.exp(cum_end), (BF16_ROWS, LANES))
    stack = jnp.concatenate([jnp.exp(cum).astype(BF16), (dt * jnp.exp(cum_end - cum)).astype(BF16),
                             *_split2(e_end)], axis=0)
    parts = ((CHUNK, 1), (CHUNK, 1), (BF16_ROWS, 2))

    for g in range(S_GROUPS):
        first = col0 + g * hpg
        cols = slice(g * gw, (g + 1) * gw)
        xg_b = x_ref[0, :, cols]
        xg = xg_b.astype(F32)
        bg = b_ref[0, :, g * S_STATE:(g + 1) * S_STATE]
        cg = c_ref[0, :, g * S_STATE:(g + 1) * S_STATE]
        cb = _dot_nt(cg, bg)
        ws = []
        for r in range(hpg):
            idx = first + r
            seg = jnp.where(mask, cum[:, idx:idx + 1] - cum_t[idx:idx + 1, :], -jnp.inf)
            ws.append((cb * jnp.exp(seg) * dt_t[idx:idx + 1, :]).astype(BF16))
        w_cat = jnp.concatenate(ws, axis=1)
        e_cum_x, e_rest_x, e_end_x = _expand_heads(stack, _head_selector(first, hpg), parts)
        bd = jnp.concatenate([jnp.where(lane_head == r, xg_b, jnp.zeros_like(xg_b))
                              for r in range(hpg)], axis=0)
        h_t = h_sc[g]
        y = _dot(w_cat, bd) + e_cum_x * _dot(cg, h_t.astype(BF16))
        wx = (xg * e_rest_x).astype(BF16)
        h_sc[g] = e_end_x[0:1] * h_t + _dot_tn(bg, wx)
        if final:
            y = y + yb_ref[0, :, cols].astype(F32) + skip_ref[:, cols] * xg
        o_ref[0, :, cols] = y.astype(o_ref.dtype)


def _ssd_pass(xs, bm, cm, dt, dt_bias, a_log, *, ncc, nlc, col0, reverse, hpg, extra=None, casts=()):
    bsz, _, ci = xs.shape
    sbc = S_GROUPS * S_STATE
    final = extra is not None

    def any_idx(b, t):
        cc = (ncc - 1 - t) if reverse else t
        lc = (nlc - 1 - (t - ncc)) if reverse else (t - ncc)
        return (b, jnp.where(t < ncc, nlc + cc, lc), 0)

    def lat_idx(b, t):
        tt = jnp.maximum(t - ncc, 0)
        return (b, (nlc - 1 - tt) if reverse else tt, 0)

    vec = pl.BlockSpec((1, LANES), lambda b, t: (0, 0))
    in_specs = [pl.BlockSpec((1, CHUNK, ci), any_idx), pl.BlockSpec((1, CHUNK, sbc), any_idx),
                pl.BlockSpec((1, CHUNK, sbc), any_idx), pl.BlockSpec((1, CHUNK, LANES), any_idx),
                vec, vec]
    args = [xs, bm, cm, dt, dt_bias, a_log]
    if final:
        in_specs += [pl.BlockSpec((1, CHUNK, ci), lat_idx), pl.BlockSpec((1, ci), lambda b, t: (0, 0))]
        args += list(extra)
    out_specs = [pl.BlockSpec((1, CHUNK, ci), lat_idx)]
    out_shape = [jax.ShapeDtypeStruct((bsz, nlc * CHUNK, ci), BF16)]
    steps = ncc + nlc
    for src, (br, bc) in casts:
        assert src.shape[0] % br == 0 and src.shape[1] % bc == 0
        nbc = src.shape[1] // bc
        n_blk = (src.shape[0] // br) * nbc
        assert n_blk <= bsz * steps, (n_blk, bsz * steps)

        def blk(b, t, nbc=nbc, n_blk=n_blk):
            s = jnp.minimum(b * steps + t, n_blk - 1)
            return s // nbc, s % nbc

        in_specs.append(pl.BlockSpec((br, bc), blk))
        args.append(src)
        out_specs.append(pl.BlockSpec((br, bc), blk))
        out_shape.append(jax.ShapeDtypeStruct(src.shape, BF16))
    outs = pl.pallas_call(
        functools.partial(_ssd_kernel, reverse=reverse, col0=col0, final=final, n_casts=len(casts),
                          hpg=hpg),
        grid=(bsz, steps),
        in_specs=in_specs,
        out_specs=out_specs,
        out_shape=out_shape,
        scratch_shapes=[pltpu.VMEM((S_GROUPS, S_STATE, hpg * S_HEAD_DIM), F32)],
        compiler_params=_params(2), name="ssd_final" if final else "ssd_bwd")(*args)
    return outs if casts else outs[0]


def _ssd_out_kernel(y_ref, z_ref, g_ref, o_ref):
    y = pltpu.einshape("crd->rcd", y_ref[0]).astype(F32)
    yt = y * _silu(z_ref[0].astype(F32))
    o_ref[0] = _rms(yt, g_ref[...]).astype(o_ref.dtype)


def _ssd_out(y_cm, oz4, norm_g, gw):
    bsz, cols, rows, ci = y_cm.shape
    rb = BF16_ROWS
    assert rows % rb == 0 and oz4.shape[-1] == 2 * ci
    zoff = ci // gw
    return pl.pallas_call(
        _ssd_out_kernel,
        grid=(bsz, rows // rb, ci // gw),
        in_specs=[pl.BlockSpec((1, cols, rb, gw), lambda b, r, g: (b, 0, r, g)),
                  pl.BlockSpec((1, rb, cols, gw), lambda b, r, g: (b, r, 0, g + zoff)),
                  pl.BlockSpec((1, gw), lambda b, r, g: (0, g))],
        out_specs=pl.BlockSpec((1, rb, cols, gw), lambda b, r, g: (b, r, 0, g)),
        out_shape=jax.ShapeDtypeStruct((bsz, rows, cols, ci), BF16),
        compiler_params=_params(3), name="ssd_out")(y_cm, oz4, norm_g)


def _pad_cols(a, n):
    return jnp.pad(a, ((0, 0), (0, n - a.shape[1])))


def kernel(x, c, ctx, c_ctx, w_ada, b_ada, norm_g, w_in, m_gate_b, m_norm_g, s_conv_w, s_conv_b,
           s_dt_bias, s_a_log, s_d, s_norm_g, w_bm, w_bs, w_out, w_ffn_gate, w_ffn_up, w_ffn_down):
    batch, seq, d_model = x.shape
    ctx_len = ctx.shape[1]
    depth = w_in.shape[0]
    rows = seq // GRID_W
    assert depth == 1 and rows * 2 == CHUNK and batch + 1 <= MOD_ROWS
    assert ctx_len % CHUNK == 0 and seq % CHUNK == 0
    m_qk = d_model // 2
    m_v = d_model
    s_inner = d_model
    s_heads = s_inner // S_HEAD_DIM
    hpg = s_heads // S_GROUPS
    gw = hpg * S_HEAD_DIM
    s_bc = S_GROUPS * S_STATE
    conv_ch = s_inner + 2 * s_bc
    n_gates = N_DIR * 2 * M_HEADS
    assert N_DIR * s_heads <= LANES and n_gates <= LANES and hpg % 2 == 0
    ncc, nlc = ctx_len // CHUNK, seq // CHUNK
    n_ctx, n_lat = batch * ctx_len, batch * seq
    t_all = seq + ctx_len
    li = 0

    cvec = jnp.zeros((MOD_ROWS, d_model), F32).at[:batch].set(c).at[batch].set(c_ctx)
    mod = _ada(cvec, w_ada[li], b_ada[li][None, :], tn=512)
    mod3 = mod.reshape(MOD_ROWS * 6, 1, d_model)
    g_n = norm_g[li]

    u = _norm_mod(x.reshape(n_lat, d_model), ctx.reshape(n_ctx, d_model), g_n[0][None, :], mod3,
                  batch, tm=min(256, n_ctx))

    wt = w_in[li].T
    tn = 512
    src, o0 = {}, 0
    for name, n in (("q", m_qk), ("k", m_qk), ("v", m_v), ("gates", n_gates), ("xbc", conv_ch),
                    ("dt", N_DIR * s_heads), ("o", m_v), ("z", s_inner), ("gm", d_model), ("gs", d_model)):
        src[name] = (o0, n)
        o0 += n
    packed = ("q", "k", "v", "xbc", "o", "z", "gm", "gs")
    offsets, dst, p0 = [], {}, 0
    for name in packed:
        s0, n = src[name]
        assert n % tn == 0
        offsets += [s0 + i * tn for i in range(n // tn)]
        dst[name] = p0
        p0 += n
    w_pk = _wprep(wt, offsets, tn, q_blocks=m_qk // tn, q_scale=float(m_qk // M_HEADS) ** -0.5)
    w_aux = _wprep(wt, [src["gates"][0], src["dt"][0]], LANES, q_blocks=0, q_scale=1.0)

    tm_tok = min(1024, n_ctx)
    tn_big = 1024
    up_blk = (min(2048, d_model), 256)
    qkv, w_gate_b, w_up_b = _matmul(
        u, w_pk, BF16, tm_tok, tn_big, 0, None, dst["q"], 2 * m_qk + m_v, name="proj_qkv",
        casts=((w_ffn_gate[li], up_blk), (w_ffn_up[li], up_blk)))
    xbc_l, w_down_b = _matmul(
        u, w_pk, BF16, tm_tok, tn_big, 0, n_lat, dst["xbc"], conv_ch, name="proj_xbc_lat",
        casts=((w_ffn_down[li], (256, d_model)),))
    xbc_c = _matmul(u, w_pk, BF16, tm_tok, tn_big, n_lat, n_ctx, dst["xbc"], conv_ch, name="proj_xbc_ctx")
    aux = _matmul(u, w_aux, F32, tm_tok, 2 * LANES, name="proj_aux")

    def side(name):
        return _Side(u, w_pk, n_lat, dst[name], 2 * d_model, min(1024, n_lat), tn_big,
                     ncc + nlc, batch * (ncc + nlc))

    gate_b = _pad_cols(m_gate_b[li].reshape(1, -1).astype(F32), LANES)
    common = dict(batch=batch, ncc=ncc, nlc=nlc)
    h_bwd, oz = _mlstm_pass(qkv, aux, gate_b, d=1, reverse=True, side=side("o"), **common)
    y_m, gmgs = _mlstm_pass(qkv, aux, gate_b, d=0, reverse=False, side=side("gm"), side_gate=True,
                            extra=(h_bwd, oz, m_norm_g[li][None, :].astype(F32)), **common)

    cw, cbias = s_conv_w[li].astype(F32), s_conv_b[li][None, :].astype(F32)
    xl4 = xbc_l.reshape(batch, rows, GRID_W, conv_ch)
    xc3 = xbc_c.reshape(batch, ctx_len, conv_ch)
    scan_in = [_conv(xl4, xc3, cw, cbias, o_, n_, 512)
               for o_, n_ in ((0, s_inner), (s_inner, s_bc), (s_inner + s_bc, s_bc))]
    scan_in.append(_dt_layout(aux, 1, batch, rows, GRID_W, ctx_len))
    dtb = _pad_cols(s_dt_bias[li].reshape(1, -1).astype(F32), LANES)
    alog = _pad_cols(s_a_log[li].reshape(1, -1).astype(F32), LANES)
    skip = jnp.repeat(s_d[li].astype(F32), S_HEAD_DIM)[None, :]
    scan = dict(ncc=ncc, nlc=nlc, hpg=hpg)
    w_blk = (LANES, d_model)
    y_bwd, w_bm_b, w_bs_b, w_out_b = _ssd_pass(
        *scan_in, dtb, alog, col0=s_heads, reverse=True,
        casts=((w_bm[li], w_blk), (w_bs[li], w_blk), (w_out[li], w_blk)), **scan)
    y_tot = _ssd_pass(*scan_in, dtb, alog, col0=0, reverse=False, extra=(y_bwd, skip), **scan)
    y_s = _ssd_out(y_tot.reshape(batch, GRID_W, rows, s_inner),
                   oz.reshape(batch, rows, GRID_W, 2 * s_inner),
                   s_norm_g[li][None, :].astype(F32), gw).reshape(n_lat, s_inner)

    tm_lat = min(1024, n_lat)
    mix_pre = _merge(y_m, y_s, w_bm_b, w_bs_b, gmgs, tm_lat, 256)
    mix = _matmul(mix_pre, w_out_b, BF16, tm_lat, tn_big, name="out_proj")
    x2 = x.reshape(n_lat, d_model)
    x_new, h_mod = _resid_mid(mix, x2, g_n[1][None, :], g_n[2][None, :], mod3, batch, tm=256)

    act = _swiglu_up(h_mod, w_gate_b, w_up_b, min(2048, n_lat), 256)
    hf = _matmul(act, w_down_b, BF16, min(512, n_lat), 512, name="ffn_down")
    out = _resid_out(hf, x_new, g_n[3][None, :], mod3, batch, tm=256)
    return out.reshape(batch, seq, d_model)
```

```python
import functools

import jax
import jax.numpy as jnp
from jax import lax
from jax.experimental import pallas as pl
from jax.experimental.pallas import tpu as pltpu

F32 = jnp.float32
BF16 = jnp.bfloat16

CHUNK = 128
GRID_W = 64
EPS = 1e-6
N_DIR = 2
M_HEADS = 8
S_GROUPS = 8
S_HEAD_DIM = 64
S_STATE = 128
S_CONV = 4
LANES = 128
BF16_ROWS = 16
MOD_ROWS = 8
W_ALIGN = 32
HEAD_INTERLEAVE = 2
VMEM_LIMIT = 56 * 1024 * 1024


def _params(n_axes):
    return pltpu.CompilerParams(dimension_semantics=("arbitrary",) * n_axes,
                                vmem_limit_bytes=VMEM_LIMIT)


def _dot(a, b):
    return jnp.dot(a, b, preferred_element_type=F32)


def _dot_nt(a, b):
    return lax.dot_general(a, b, (((1,), (1,)), ((), ())), preferred_element_type=F32)


def _dot_tn(a, b):
    return lax.dot_general(a, b, (((0,), (0,)), ((), ())), preferred_element_type=F32)


def _cumsum_time(tri, x):
    hi = x.astype(BF16)
    r1 = x - hi.astype(F32)
    mid = r1.astype(BF16)
    lo = (r1 - mid.astype(F32)).astype(BF16)
    return _dot(tri, hi) + _dot(tri, mid) + _dot(tri, lo)


def _sigmoid(x):
    return 1.0 / (1.0 + jnp.exp(-x))


def _silu(x):
    return x * _sigmoid(x)


def _softplus(x):
    return jnp.maximum(x, 0.0) + jnp.log(1.0 + jnp.exp(-jnp.abs(x)))


def _time_masks(reverse):
    jj = lax.broadcasted_iota(jnp.int32, (CHUNK, CHUNK), 0)
    ss = lax.broadcasted_iota(jnp.int32, (CHUNK, CHUNK), 1)
    mask = (ss >= jj) if reverse else (ss <= jj)
    return mask, jnp.where(mask, 1.0, 0.0).astype(BF16)


def _mm_kernel(a_ref, w_ref, *rest, n_casts):
    srcs, o_ref, dsts = rest[:n_casts], rest[n_casts], rest[n_casts + 1:]
    o_ref[...] = _dot(a_ref[...].astype(BF16), w_ref[...].astype(BF16)).astype(o_ref.dtype)
    for src, dst in zip(srcs, dsts):
        dst[...] = src[...].astype(dst.dtype)


def _matmul(a, w, out_dtype, tm, tn, row_off=0, rows=None, col_off=0, cols=None, name="matmul",
            casts=()):
    k = a.shape[1]
    rows = a.shape[0] - row_off if rows is None else rows
    cols = w.shape[1] - col_off if cols is None else cols
    assert rows % tm == 0 and cols % tn == 0 and row_off % tm == 0 and col_off % tn == 0
    roff, coff = row_off // tm, col_off // tn
    ni, nj = rows // tm, cols // tn
    in_specs = [pl.BlockSpec((tm, k), lambda i, j: (i + roff, 0)),
                pl.BlockSpec((k, tn), lambda i, j: (0, j + coff))]
    out_specs = [pl.BlockSpec((tm, tn), lambda i, j: (i, j))]
    out_shape = [jax.ShapeDtypeStruct((rows, cols), out_dtype)]
    for src, (br, bc) in casts:
        assert src.shape[0] % br == 0 and src.shape[1] % bc == 0
        nbc = src.shape[1] // bc
        n_blk = (src.shape[0] // br) * nbc
        assert n_blk <= ni * nj, (n_blk, ni * nj)

        def blk(i, j, nbc=nbc, n_blk=n_blk):
            s = jnp.minimum(i * nj + j, n_blk - 1)
            return s // nbc, s % nbc

        in_specs.append(pl.BlockSpec((br, bc), blk))
        out_specs.append(pl.BlockSpec((br, bc), blk))
        out_shape.append(jax.ShapeDtypeStruct(src.shape, BF16))
    outs = pl.pallas_call(
        functools.partial(_mm_kernel, n_casts=len(casts)),
        grid=(ni, nj),
        in_specs=in_specs,
        out_specs=out_specs,
        out_shape=out_shape,
        compiler_params=_params(2), name=name)(a, w, *[src for src, _ in casts])
    return outs if casts else outs[0]


def _wprep_kernel(off_ref, w_ref, o_ref, *, q_blocks, q_scale):
    del off_ref
    scale = jnp.where(pl.program_id(0) < q_blocks, q_scale, 1.0)
    o_ref[...] = (w_ref[...] * scale).T.astype(BF16)


def _wprep(wt, offsets, tn, q_blocks, q_scale):
    k = wt.shape[1]
    nblk = len(offsets)
    assert all(o % W_ALIGN == 0 for o in offsets)
    return pl.pallas_call(
        functools.partial(_wprep_kernel, q_blocks=q_blocks, q_scale=q_scale),
        grid_spec=pltpu.PrefetchScalarGridSpec(
            num_scalar_prefetch=1, grid=(nblk,),
            in_specs=[pl.BlockSpec((pl.Element(tn), pl.Element(k)),
                                   lambda j, off: (pl.multiple_of(off[j], W_ALIGN), 0))],
            out_specs=pl.BlockSpec((k, tn), lambda j, off: (0, j))),
        out_shape=jax.ShapeDtypeStruct((k, nblk * tn), BF16),
        compiler_params=_params(1), name="w_in_prep")(jnp.asarray(offsets, jnp.int32), wt)


def _ada_kernel(c_ref, w_ref, b_ref, o_ref):
    a = _silu(c_ref[...]).astype(BF16)
    o_ref[...] = _dot(a, w_ref[...].astype(BF16)) + b_ref[...]


def _ada(cvec, w, b, tn):
    k, n = w.shape
    return pl.pallas_call(
        _ada_kernel,
        grid=(n // tn,),
        in_specs=[pl.BlockSpec((MOD_ROWS, k), lambda j: (0, 0)),
                  pl.BlockSpec((k, tn), lambda j: (0, j)),
                  pl.BlockSpec((1, tn), lambda j: (0, j))],
        out_specs=pl.BlockSpec((MOD_ROWS, tn), lambda j: (0, j)),
        out_shape=jax.ShapeDtypeStruct((MOD_ROWS, n), F32),
        compiler_params=_params(1), name="ada")(cvec, w, b)


def _merge_kernel(ym_ref, ys_ref, wm_ref, ws_ref, gm_ref, gs_ref, o_ref):
    am = _dot(ym_ref[...], wm_ref[...])
    a_s = _dot(ys_ref[...], ws_ref[...])
    o_ref[...] = (gm_ref[...].astype(F32) * am + gs_ref[...].astype(F32) * a_s).astype(o_ref.dtype)


def _merge(ym, ys, wm, ws, gates, tm, tn):
    m, k = ym.shape
    n = wm.shape[1]
    assert gates.shape[1] == 2 * n
    a_spec = pl.BlockSpec((tm, k), lambda i, j: (i, 0))
    w_spec = pl.BlockSpec((k, tn), lambda i, j: (0, j))
    return pl.pallas_call(
        _merge_kernel,
        grid=(m // tm, n // tn),
        in_specs=[a_spec, a_spec, w_spec, w_spec,
                  pl.BlockSpec((tm, tn), lambda i, j: (i, j)),
                  pl.BlockSpec((tm, tn), lambda i, j: (i, j + n // tn))],
        out_specs=pl.BlockSpec((tm, tn), lambda i, j: (i, j)),
        out_shape=jax.ShapeDtypeStruct((m, n), BF16),
        compiler_params=_params(2), name="merge")(ym, ys, wm, ws, gates, gates)


def _swiglu_kernel(a_ref, wg_ref, wu_ref, o_ref):
    a = a_ref[...]
    o_ref[...] = (_silu(_dot(a, wg_ref[...])) * _dot(a, wu_ref[...])).astype(o_ref.dtype)


def _swiglu_up(a, wg, wu, tm, tn):
    m, k = a.shape
    n = wg.shape[1]
    w_spec = pl.BlockSpec((k, tn), lambda i, j: (0, j))
    return pl.pallas_call(
        _swiglu_kernel,
        grid=(m // tm, n // tn),
        in_specs=[pl.BlockSpec((tm, k), lambda i, j: (i, 0)), w_spec, w_spec],
        out_specs=pl.BlockSpec((tm, tn), lambda i, j: (i, j)),
        out_shape=jax.ShapeDtypeStruct((m, n), BF16),
        compiler_params=_params(2), name="swiglu_up")(a, wg, wu)


def _rms(x, g):
    return x * lax.rsqrt(jnp.mean(x * x, axis=-1, keepdims=True) + EPS) * g


def _norm_mod_kernel(xl_ref, xc_ref, g_ref, sh_ref, sc_ref, o_ref, *, n_lat_blocks):
    i = pl.program_id(0)

    def emit(x):
        o_ref[...] = (_rms(x, g_ref[...]) * (1.0 + sc_ref[0]) + sh_ref[0]).astype(o_ref.dtype)

    @pl.when(i < n_lat_blocks)
    def _():
        emit(xl_ref[...])

    @pl.when(i >= n_lat_blocks)
    def _():
        emit(xc_ref[...])


def _norm_mod(xl, xc, g, mod3, batch, tm):
    d = xl.shape[1]
    nlb = xl.shape[0] // tm
    ncb = xc.shape[0] // tm
    per_batch = nlb // batch

    def mod_row(i):
        return jnp.where(i < nlb, jnp.minimum(i, nlb - 1) // per_batch, batch)

    return pl.pallas_call(
        functools.partial(_norm_mod_kernel, n_lat_blocks=nlb),
        grid=(nlb + ncb,),
        in_specs=[pl.BlockSpec((tm, d), lambda i: (jnp.minimum(i, nlb - 1), 0)),
                  pl.BlockSpec((tm, d), lambda i: (jnp.maximum(i - nlb, 0), 0)),
                  pl.BlockSpec((1, d), lambda i: (0, 0)),
                  pl.BlockSpec((1, 1, d), lambda i: (mod_row(i) * 6 + 0, 0, 0)),
                  pl.BlockSpec((1, 1, d), lambda i: (mod_row(i) * 6 + 1, 0, 0))],
        out_specs=pl.BlockSpec((tm, d), lambda i: (i, 0)),
        out_shape=jax.ShapeDtypeStruct(((nlb + ncb) * tm, d), BF16),
        compiler_params=_params(1), name="norm_mod")(xl, xc, g, mod3, mod3)


def _resid_mid_kernel(mix_ref, x_ref, g1_ref, g2_ref, ga_ref, sh_ref, sc_ref, xn_ref, h_ref):
    xn = x_ref[...] + ga_ref[0] * _rms(mix_ref[...].astype(F32), g1_ref[...])
    xn_ref[...] = xn
    h_ref[...] = (_rms(xn, g2_ref[...]) * (1.0 + sc_ref[0]) + sh_ref[0]).astype(h_ref.dtype)


def _resid_mid(mix, x, g1, g2, mod3, batch, tm):
    m, d = x.shape
    per_batch = m // tm // batch
    row = pl.BlockSpec((tm, d), lambda i: (i, 0))
    vec = pl.BlockSpec((1, d), lambda i: (0, 0))

    def mod(which):
        return pl.BlockSpec((1, 1, d), lambda i: ((i // per_batch) * 6 + which, 0, 0))

    return pl.pallas_call(
        _resid_mid_kernel,
        grid=(m // tm,),
        in_specs=[row, row, vec, vec, mod(2), mod(3), mod(4)],
        out_specs=[row, row],
        out_shape=[jax.ShapeDtypeStruct((m, d), F32), jax.ShapeDtypeStruct((m, d), BF16)],
        compiler_params=_params(1), name="resid_mid")(mix, x, g1, g2, mod3, mod3, mod3)


def _resid_out_kernel(hf_ref, x_ref, g_ref, ga_ref, o_ref):
    o_ref[...] = x_ref[...] + ga_ref[0] * _rms(hf_ref[...].astype(F32), g_ref[...])


def _resid_out(hf, x, g, mod3, batch, tm):
    m, d = x.shape
    per_batch = m // tm // batch
    row = pl.BlockSpec((tm, d), lambda i: (i, 0))
    return pl.pallas_call(
        _resid_out_kernel,
        grid=(m // tm,),
        in_specs=[row, row, pl.BlockSpec((1, d), lambda i: (0, 0)),
                  pl.BlockSpec((1, 1, d), lambda i: ((i // per_batch) * 6 + 5, 0, 0))],
        out_specs=row,
        out_shape=jax.ShapeDtypeStruct((m, d), F32),
        compiler_params=_params(1), name="resid_out")(hf, x, g, mod3)


SIDE_PIECES = 8


class _Side:
    def __init__(self, a, w, rows, col_off, cols, tm, tn, steps_per_batch, n_steps):
        assert rows % tm == 0 and cols % tn == 0 and col_off % tn == 0
        self.a, self.w, self.rows, self.cols, self.tm, self.tn = a, w, rows, cols, tm, tn
        self.nj, self.coff, self.spb = cols // tn, col_off // tn, steps_per_batch
        self.n_tiles = (rows // tm) * self.nj
        assert self.n_tiles <= n_steps, (self.n_tiles, n_steps)

    def _tile(self, b, t):
        s = jnp.minimum(b * self.spb + t, self.n_tiles - 1)
        return s // self.nj, s % self.nj

    def in_specs(self):
        k = self.a.shape[1]
        return [pl.BlockSpec((self.tm, k), lambda b, t: (self._tile(b, t)[0], 0)),
                pl.BlockSpec((k, self.tn), lambda b, t: (0, self._tile(b, t)[1] + self.coff))]

    def out_spec(self):
        return pl.BlockSpec((self.tm, self.tn), lambda b, t: self._tile(b, t))

    def out_shape(self):
        return jax.ShapeDtypeStruct((self.rows, self.cols), BF16)


def _side_piece(a_ref, w_ref, o_ref, p, swish):
    tm, tn = o_ref.shape
    strips = SIDE_PIECES // 2
    rs = slice((p // strips) * (tm // 2), (p // strips + 1) * (tm // 2))
    cs = slice((p % strips) * (tn // strips), (p % strips + 1) * (tn // strips))
    val = _dot(a_ref[rs, :], w_ref[:, cs])
    o_ref[rs, cs] = (_sigmoid(val) * jnp.where(swish, val, 1.0)).astype(o_ref.dtype)


def _mlstm_kernel(*refs, reverse, d, final, side, dk, dv):
    n_in = 5 + (3 if final else 0)
    scan_in, refs = refs[:n_in], refs[n_in:]
    if side:
        (sa_ref, sw_ref, o_ref, so_ref), refs = refs[:4], refs[4:]
        swish_half, spb, n_tiles, nj = side
        tile = jnp.minimum(pl.program_id(0) * spb + pl.program_id(1), n_tiles - 1)
        swish = (tile % nj >= nj // 2) if swish_half else False
    else:
        o_ref, refs = refs[0], refs[1:]
    ct_sc, n_sc, m_sc = refs
    q_ref, k_ref, v_ref, g_ref, gb_ref = scan_in[:5]
    if final:
        hb_ref, og_ref, hg_ref = scan_in[5:]

    @pl.when(pl.program_id(1) == 0)
    def _():
        ct_sc[...] = jnp.zeros_like(ct_sc)
        n_sc[...] = jnp.zeros_like(n_sc)
        m_sc[...] = jnp.zeros_like(m_sc)

    mask, tri = _time_masks(reverse)
    last = 0 if reverse else CHUNK - 1

    g = g_ref[...] + gb_ref[...]
    log_f = jnp.minimum(g, 0.0) - jnp.log(1.0 + jnp.exp(-jnp.abs(g)))
    csum = _cumsum_time(tri, log_f)
    off_i = d * 2 * M_HEADS
    off_f = off_i + M_HEADS
    li_t = g if off_i == 0 else pltpu.roll(g, LANES - off_i, axis=1)
    b_t = pltpu.roll(csum, LANES - off_f, axis=1)
    r_t = (li_t - b_t).T

    def gate_stage(h):
        b_c = b_t[:, h:h + 1]
        logd = jnp.where(mask, b_c + r_t[h:h + 1, :], -jnp.inf)
        m_prev = m_sc[h:h + 1, 0:1]
        m_carry = b_c + m_prev
        m_row = jnp.maximum(m_carry, jnp.max(logd, axis=1, keepdims=True))
        return dict(b_c=b_c, li_c=li_t[:, h:h + 1], m_prev=m_prev, m_row=m_row,
                    d_mat=jnp.exp(logd - m_row), w_carry=jnp.exp(m_carry - m_row),
                    q=q_ref[:, h * dk:(h + 1) * dk], k=k_ref[:, h * dk:(h + 1) * dk],
                    v=v_ref[:, h * dv:(h + 1) * dv], ct=ct_sc[h], n_row=n_sc[h:h + 1, :])

    def out_stage(h, st):
        s = st["s"]
        num = _dot(s.astype(BF16), st["v"]) + st["w_carry"] * _dot(st["q"], st["ct"].astype(BF16))
        den = (jnp.sum(s, axis=1, keepdims=True)
               + st["w_carry"] * jnp.sum(st["q"].astype(F32) * st["n_row"], axis=1, keepdims=True))
        h_out = num / jnp.maximum(jnp.abs(den), jnp.exp(-st["m_row"]))
        cols = slice(h * dv, (h + 1) * dv)
        if final:
            hn = _rms(h_out + hb_ref[:, cols].astype(F32), hg_ref[:, cols])
            o_ref[:, cols] = (hn * og_ref[:, cols].astype(F32)).astype(o_ref.dtype)
        else:
            o_ref[:, cols] = h_out.astype(o_ref.dtype)

    def state_stage(h, st):
        b_c, k, v = st["b_c"], st["k"], st["v"]
        m_end = st["m_row"][last:last + 1, :]
        b_end = b_c[last:last + 1, :]
        w_state = jnp.exp(b_end - b_c + st["li_c"] - m_end)
        decay = jnp.exp(b_end + st["m_prev"] - m_end)
        ct_sc[h] = decay * st["ct"] + _dot_tn(k, (w_state * v.astype(F32)).astype(BF16))
        n_sc[h:h + 1, :] = decay * st["n_row"] + jnp.sum(w_state * k.astype(F32), axis=0, keepdims=True)
        m_sc[h:h + 1, :] = jnp.broadcast_to(m_end, (1, LANES))

    for h0 in range(0, M_HEADS, HEAD_INTERLEAVE):
        heads = range(h0, h0 + HEAD_INTERLEAVE)
        st = {h: gate_stage(h) for h in heads}
        for h in heads:
            st[h]["s"] = _dot_nt(st[h]["q"], st[h]["k"]) * st[h]["d_mat"]
        for h in heads:
            out_stage(h, st[h])
        if side:
            _side_piece(sa_ref, sw_ref, so_ref, 2 * (h0 // HEAD_INTERLEAVE), swish)
        for h in heads:
            state_stage(h, st[h])
        if side:
            _side_piece(sa_ref, sw_ref, so_ref, 2 * (h0 // HEAD_INTERLEAVE) + 1, swish)


def _mlstm_pass(qkv, gates, gate_b, *, batch, ncc, nlc, d, reverse, extra=None, side=None,
                side_swish_half=False):
    assert 2 * (M_HEADS // HEAD_INTERLEAVE) == SIDE_PIECES
    hv = qkv.shape[1] // 2
    hk = hv // 2
    dk, dv = hk // M_HEADS, hv // M_HEADS
    final = extra is not None

    def comb_idx(b, t):
        cc = (ncc - 1 - t) if reverse else t
        lc = (nlc - 1 - (t - ncc)) if reverse else (t - ncc)
        return jnp.where(t < ncc, batch * nlc + b * ncc + cc, b * nlc + lc)

    def lat_idx(b, t):
        tt = jnp.maximum(t - ncc, 0)
        return b * nlc + ((nlc - 1 - tt) if reverse else tt)

    lat_spec = pl.BlockSpec((CHUNK, hv), lambda b, t: (lat_idx(b, t), 0))
    in_specs = [pl.BlockSpec((CHUNK, hk), lambda b, t: (comb_idx(b, t), 0)),
                pl.BlockSpec((CHUNK, hk), lambda b, t: (comb_idx(b, t), 1)),
                pl.BlockSpec((CHUNK, hv), lambda b, t: (comb_idx(b, t), 1)),
                pl.BlockSpec((CHUNK, LANES), lambda b, t: (comb_idx(b, t), 0)),
                pl.BlockSpec((1, LANES), lambda b, t: (0, 0))]
    args = [qkv, qkv, qkv, gates, gate_b]
    if final:
        in_specs += [lat_spec, lat_spec, pl.BlockSpec((1, hv), lambda b, t: (0, 0))]
        args += list(extra)
    out_specs = [lat_spec]
    out_shape = [jax.ShapeDtypeStruct((batch * nlc * CHUNK, hv), BF16)]
    if side:
        in_specs += side.in_specs()
        args += [side.a, side.w]
        out_specs.append(side.out_spec())
        out_shape.append(side.out_shape())
    outs = pl.pallas_call(
        functools.partial(_mlstm_kernel, reverse=reverse, d=d, final=final, dk=dk, dv=dv,
                          side=(side_swish_half, side.spb, side.n_tiles, side.nj) if side else None),
        grid=(batch, ncc + nlc),
        in_specs=in_specs,
        out_specs=out_specs,
        out_shape=out_shape,
        scratch_shapes=[pltpu.VMEM((M_HEADS, dk, dv), F32),
                        pltpu.VMEM((M_HEADS, dk), F32),
                        pltpu.VMEM((M_HEADS, LANES), F32)],
        compiler_params=_params(2), name="mlstm_final" if final else "mlstm_bwd")(*args)
    return outs if side else outs[0]


def _store_col_major(o_ref, scr):
    rows, cols, _ = scr.shape
    step = BF16_ROWS
    for p in range(cols // step):
        piece = pltpu.einshape("rcd->crd", scr[:, p * step:(p + 1) * step, :])
        o_ref[0, p * step * rows:(p + 1) * step * rows, :] = piece.reshape(step * rows, piece.shape[-1])


def _conv_kernel(x_ref, xc_ref, w_ref, b_ref, o_ref, scr):
    rows, cols = x_ref.shape[1], x_ref.shape[2]
    tlen = xc_ref.shape[1]
    w = w_ref[...]
    w0, w1, w2, w3, bias = w[0:1, :], w[1:2, :], w[2:3, :], w[3:4, :], b_ref[...]
    cidx = lax.broadcasted_iota(jnp.int32, (cols, 1), 0)

    xc = xc_ref[0].astype(F32)
    tidx = lax.broadcasted_iota(jnp.int32, (tlen, 1), 0)
    prev = jnp.where(tidx >= 1, pltpu.roll(xc, 1, axis=0), 0.0)
    nxt1 = jnp.where(tidx <= tlen - 2, pltpu.roll(xc, tlen - 1, axis=0), 0.0)
    nxt2 = jnp.where(tidx <= tlen - 3, pltpu.roll(xc, tlen - 2, axis=0), 0.0)
    o_ref[0, rows * cols:rows * cols + tlen, :] = _silu(
        w0 * prev + w1 * xc + w2 * nxt1 + w3 * nxt2 + bias).astype(o_ref.dtype)

    def ld(r):
        return x_ref[0, r].astype(F32)

    def from_prev_col(a):
        return jnp.where(cidx >= 1, pltpu.roll(a, 1, axis=0), 0.0)

    def from_next_col(a):
        return jnp.where(cidx <= cols - 2, pltpu.roll(a, cols - 1, axis=0), 0.0)

    def emit(r, prev, cur, nxt1, nxt2):
        scr[r] = _silu(w0 * prev + w1 * cur + w2 * nxt1 + w3 * nxt2 + bias).astype(scr.dtype)

    emit(0, from_prev_col(ld(rows - 1)), ld(0), ld(1), ld(2))

    def body(r, carry):
        emit(r, ld(r - 1), ld(r), ld(r + 1), ld(r + 2))
        return carry

    lax.fori_loop(1, rows - 2, body, 0)
    top0 = from_next_col(ld(0))
    emit(rows - 2, ld(rows - 3), ld(rows - 2), ld(rows - 1), top0)
    emit(rows - 1, ld(rows - 2), ld(rows - 1), top0, from_next_col(ld(1)))
    _store_col_major(o_ref, scr)


def _conv(x4, x3, w, b, ch_off, ch_n, cb):
    assert ch_off % cb == 0 and ch_n % cb == 0
    off = ch_off // cb
    bsz, rows, cols, _ = x4.shape
    tlen = x3.shape[1]
    return pl.pallas_call(
        _conv_kernel,
        grid=(bsz, ch_n // cb),
        in_specs=[pl.BlockSpec((1, rows, cols, cb), lambda b_, j: (b_, 0, 0, j + off)),
                  pl.BlockSpec((1, tlen, cb), lambda b_, j: (b_, 0, j + off)),
                  pl.BlockSpec((S_CONV, cb), lambda b_, j: (0, j + off)),
                  pl.BlockSpec((1, cb), lambda b_, j: (0, j + off))],
        out_specs=pl.BlockSpec((1, rows * cols + tlen, cb), lambda b_, j: (b_, 0, j)),
        out_shape=jax.ShapeDtypeStruct((bsz, rows * cols + tlen, ch_n), BF16),
        scratch_shapes=[pltpu.VMEM((rows, cols, cb), BF16)],
        compiler_params=_params(2), name="conv")(x4, x3, w, b)


def _dt_layout_kernel(x_ref, xc_ref, o_ref, *, rows, cols):
    _store_col_major(o_ref, x_ref[...].reshape(rows, cols, x_ref.shape[1]))
    o_ref[0, rows * cols:rows * cols + xc_ref.shape[0], :] = xc_ref[...]


def _dt_layout(aux, col, bsz, rows, cols, tlen):
    t_lat = rows * cols
    assert (bsz * t_lat) % tlen == 0
    ctx0 = bsz * t_lat // tlen
    return pl.pallas_call(
        functools.partial(_dt_layout_kernel, rows=rows, cols=cols),
        grid=(bsz,),
        in_specs=[pl.BlockSpec((t_lat, LANES), lambda b_: (b_, col)),
                  pl.BlockSpec((tlen, LANES), lambda b_: (ctx0 + b_, col))],
        out_specs=pl.BlockSpec((1, t_lat + tlen, LANES), lambda b_: (b_, 0, 0)),
        out_shape=jax.ShapeDtypeStruct((bsz, t_lat + tlen, LANES), F32),
        compiler_params=_params(1), name="dt_layout")(aux, aux)


def _split2(x):
    hi = x.astype(BF16)
    return hi, (x - hi.astype(F32)).astype(BF16)


def _head_selector(first, hpg):
    gw = hpg * S_HEAD_DIM
    row = lax.broadcasted_iota(jnp.int32, (LANES, gw), 0)
    head = lax.broadcasted_iota(jnp.int32, (LANES, gw), 1) // S_HEAD_DIM
    return jnp.where(row == first + head, 1.0, 0.0).astype(BF16)


def _expand_heads(stack, sel, parts):
    full = _dot(stack, sel)
    out, r0 = [], 0
    for rows, n_pieces in parts:
        acc = full[r0:r0 + rows]
        for p in range(1, n_pieces):
            acc = acc + full[r0 + p * rows:r0 + (p + 1) * rows]
        out.append(acc)
        r0 += n_pieces * rows
    return out


def _ssd_kernel(*refs, reverse, col0, final, n_casts, hpg):
    n_in = 6 + (2 if final else 0)
    scan_in, refs = refs[:n_in], refs[n_in:]
    cast_src, o_ref, cast_dst, h_sc = (refs[:n_casts], refs[n_casts],
                                       refs[n_casts + 1:2 * n_casts + 1], refs[2 * n_casts + 1])
    for src, dst in zip(cast_src, cast_dst):
        dst[...] = src[...].astype(dst.dtype)
    x_ref, b_ref, c_ref, dt_ref, dtb_ref, alog_ref = scan_in[:6]
    if final:
        yb_ref, skip_ref = scan_in[6:]
    gw = hpg * S_HEAD_DIM

    @pl.when(pl.program_id(1) == 0)
    def _():
        h_sc[...] = jnp.zeros_like(h_sc)

    mask, tri = _time_masks(reverse)
    last = 0 if reverse else CHUNK - 1
    lane_head = lax.broadcasted_iota(jnp.int32, (CHUNK, gw), 1) // S_HEAD_DIM

    dt = _softplus(dt_ref[0] + dtb_ref[...])
    cum = _cumsum_time(tri, -dt * jnp.exp(alog_ref[...]))
    cum_t = cum.T
    dt_t = dt.T
    cum_end = cum[last:last + 1, :]
    e_end = jnp.broadcast_to(jnp.exp(cum_end), (BF16_ROWS, LANES))
    stack = jnp.concatenate([jnp.exp(cum).astype(BF16), (dt * jnp.exp(cum_end - cum)).astype(BF16),
                             *_split2(e_end)], axis=0)
    parts = ((CHUNK, 1), (CHUNK, 1), (BF16_ROWS, 2))

    for g in range(S_GROUPS):
        first = col0 + g * hpg
        cols = slice(g * gw, (g + 1) * gw)
        xg_b = x_ref[0, :, cols]
        xg = xg_b.astype(F32)
        bg = b_ref[0, :, g * S_STATE:(g + 1) * S_STATE]
        cg = c_ref[0, :, g * S_STATE:(g + 1) * S_STATE]
        cb = _dot_nt(cg, bg)
        ws = []
        for r in range(hpg):
            idx = first + r
            seg = jnp.where(mask, cum[:, idx:idx + 1] - cum_t[idx:idx + 1, :], -jnp.inf)
            ws.append((cb * jnp.exp(seg) * dt_t[idx:idx + 1, :]).astype(BF16))
        w_cat = jnp.concatenate(ws, axis=1)
        e_cum_x, e_rest_x, e_end_x = _expand_heads(stack, _head_selector(first, hpg), parts)
        bd = jnp.concatenate([jnp.where(lane_head == r, xg_b, jnp.zeros_like(xg_b))
                              for r in range(hpg)], axis=0)
        h_t = h_sc[g]
        y = _dot(w_cat, bd) + e_cum_x * _dot(cg, h_t.astype(BF16))
        wx = (xg * e_rest_x).astype(BF16)
        h_sc[g] = e_end_x[0:1] * h_t + _dot_tn(bg, wx)
        if final:
            y = y + yb_ref[0, :, cols].astype(F32) + skip_ref[:, cols] * xg
        o_ref[0, :, cols] = y.astype(o_ref.dtype)


def _ssd_pass(xs, bm, cm, dt, dt_bias, a_log, *, ncc, nlc, col0, reverse, hpg, extra=None, casts=()):
    bsz, _, ci = xs.shape
    sbc = S_GROUPS * S_STATE
    final = extra is not None

    def any_idx(b, t):
        cc = (ncc - 1 - t) if reverse else t
        lc = (nlc - 1 - (t - ncc)) if reverse else (t - ncc)
        return (b, jnp.where(t < ncc, nlc + cc, lc), 0)

    def lat_idx(b, t):
        tt = jnp.maximum(t - ncc, 0)
        return (b, (nlc - 1 - tt) if reverse else tt, 0)

    vec = pl.BlockSpec((1, LANES), lambda b, t: (0, 0))
    in_specs = [pl.BlockSpec((1, CHUNK, ci), any_idx), pl.BlockSpec((1, CHUNK, sbc), any_idx),
                pl.BlockSpec((1, CHUNK, sbc), any_idx), pl.BlockSpec((1, CHUNK, LANES), any_idx),
                vec, vec]
    args = [xs, bm, cm, dt, dt_bias, a_log]
    if final:
        in_specs += [pl.BlockSpec((1, CHUNK, ci), lat_idx), pl.BlockSpec((1, ci), lambda b, t: (0, 0))]
        args += list(extra)
    out_specs = [pl.BlockSpec((1, CHUNK, ci), lat_idx)]
    out_shape = [jax.ShapeDtypeStruct((bsz, nlc * CHUNK, ci), BF16)]
    steps = ncc + nlc
    for src, (br, bc) in casts:
        assert src.shape[0] % br == 0 and src.shape[1] % bc == 0
        nbc = src.shape[1] // bc
        n_blk = (src.shape[0] // br) * nbc
        assert n_blk <= bsz * steps, (n_blk, bsz * steps)

        def blk(b, t, nbc=nbc, n_blk=n_blk):
            s = jnp.minimum(b * steps + t, n_blk - 1)
            return s // nbc, s % nbc

        in_specs.append(pl.BlockSpec((br, bc), blk))
        args.append(src)
        out_specs.append(pl.BlockSpec((br, bc), blk))
        out_shape.append(jax.ShapeDtypeStruct(src.shape, BF16))
    outs = pl.pallas_call(
        functools.partial(_ssd_kernel, reverse=reverse, col0=col0, final=final, n_casts=len(casts),
                          hpg=hpg),
        grid=(bsz, steps),
        in_specs=in_specs,
        out_specs=out_specs,
        out_shape=out_shape,
        scratch_shapes=[pltpu.VMEM((S_GROUPS, S_STATE, hpg * S_HEAD_DIM), F32)],
        compiler_params=_params(2), name="ssd_final" if final else "ssd_bwd")(*args)
    return outs if casts else outs[0]


def _ssd_out_kernel(y_ref, z_ref, g_ref, o_ref):
    y = pltpu.einshape("crd->rcd", y_ref[0]).astype(F32)
    yt = y * z_ref[0].astype(F32)
    o_ref[0] = _rms(yt, g_ref[...]).astype(o_ref.dtype)


def _ssd_out(y_cm, oz4, norm_g, gw):
    bsz, cols, rows, ci = y_cm.shape
    rb = 2 * BF16_ROWS
    assert rows % rb == 0 and oz4.shape[-1] == 2 * ci
    zoff = ci // gw
    return pl.pallas_call(
        _ssd_out_kernel,
        grid=(bsz, rows // rb, ci // gw),
        in_specs=[pl.BlockSpec((1, cols, rb, gw), lambda b, r, g: (b, 0, r, g)),
                  pl.BlockSpec((1, rb, cols, gw), lambda b, r, g: (b, r, 0, g + zoff)),
                  pl.BlockSpec((1, gw), lambda b, r, g: (0, g))],
        out_specs=pl.BlockSpec((1, rb, cols, gw), lambda b, r, g: (b, r, 0, g)),
        out_shape=jax.ShapeDtypeStruct((bsz, rows, cols, ci), BF16),
        compiler_params=_params(3), name="ssd_out")(y_cm, oz4, norm_g)


def _pad_cols(a, n):
    return jnp.pad(a, ((0, 0), (0, n - a.shape[1])))


def kernel(x, c, ctx, c_ctx, w_ada, b_ada, norm_g, w_in, m_gate_b, m_norm_g, s_conv_w, s_conv_b,
           s_dt_bias, s_a_log, s_d, s_norm_g, w_bm, w_bs, w_out, w_ffn_gate, w_ffn_up, w_ffn_down):
    batch, seq, d_model = x.shape
    ctx_len = ctx.shape[1]
    depth = w_in.shape[0]
    rows = seq // GRID_W
    assert depth == 1 and rows * 2 == CHUNK and batch + 1 <= MOD_ROWS
    assert ctx_len % CHUNK == 0 and seq % CHUNK == 0
    m_qk = d_model // 2
    m_v = d_model
    s_inner = d_model
    s_heads = s_inner // S_HEAD_DIM
    hpg = s_heads // S_GROUPS
    gw = hpg * S_HEAD_DIM
    s_bc = S_GROUPS * S_STATE
    conv_ch = s_inner + 2 * s_bc
    n_gates = N_DIR * 2 * M_HEADS
    assert N_DIR * s_heads <= LANES and n_gates <= LANES and hpg % 2 == 0
    ncc, nlc = ctx_len // CHUNK, seq // CHUNK
    n_ctx, n_lat = batch * ctx_len, batch * seq
    t_all = seq + ctx_len
    li = 0

    cvec = jnp.zeros((MOD_ROWS, d_model), F32).at[:batch].set(c).at[batch].set(c_ctx)
    mod = _ada(cvec, w_ada[li], b_ada[li][None, :], tn=512)
    mod3 = mod.reshape(MOD_ROWS * 6, 1, d_model)
    g_n = norm_g[li]

    u = _norm_mod(x.reshape(n_lat, d_model), ctx.reshape(n_ctx, d_model), g_n[0][None, :], mod3,
                  batch, tm=min(512, n_ctx))

    wt = w_in[li].T
    tn = 512
    src, o0 = {}, 0
    for name, n in (("q", m_qk), ("k", m_qk), ("v", m_v), ("gates", n_gates), ("xbc", conv_ch),
                    ("dt", N_DIR * s_heads), ("o", m_v), ("z", s_inner), ("gm", d_model), ("gs", d_model)):
        src[name] = (o0, n)
        o0 += n
    packed = ("q", "k", "v", "xbc", "o", "z", "gm", "gs")
    offsets, dst, p0 = [], {}, 0
    for name in packed:
        s0, n = src[name]
        assert n % tn == 0
        offsets += [s0 + i * tn for i in range(n // tn)]
        dst[name] = p0
        p0 += n
    w_pk = _wprep(wt, offsets, tn, q_blocks=m_qk // tn, q_scale=float(m_qk // M_HEADS) ** -0.5)
    w_aux = _wprep(wt, [src["gates"][0], src["dt"][0]], LANES, q_blocks=0, q_scale=1.0)

    tm_tok = min(1024, n_ctx)
    tn_big = 1024
    up_blk = (min(2048, d_model), 256)
    qkv, w_gate_b, w_up_b = _matmul(
        u, w_pk, BF16, tm_tok, tn_big, 0, None, dst["q"], 2 * m_qk + m_v, name="proj_qkv",
        casts=((w_ffn_gate[li], up_blk), (w_ffn_up[li], up_blk)))
    xbc_l, w_down_b = _matmul(
        u, w_pk, BF16, tm_tok, tn_big, 0, n_lat, dst["xbc"], conv_ch, name="proj_xbc_lat",
        casts=((w_ffn_down[li], (256, d_model)),))
    xbc_c = _matmul(u, w_pk, BF16, tm_tok, tn_big, n_lat, n_ctx, dst["xbc"], conv_ch, name="proj_xbc_ctx")
    aux = _matmul(u, w_aux, F32, tm_tok, 2 * LANES, name="proj_aux")

    def side(name):
        return _Side(u, w_pk, n_lat, dst[name], 2 * d_model, min(1024, n_lat), tn_big,
                     ncc + nlc, batch * (ncc + nlc))

    gate_b = _pad_cols(m_gate_b[li].reshape(1, -1).astype(F32), LANES)
    common = dict(batch=batch, ncc=ncc, nlc=nlc)
    h_bwd, oz = _mlstm_pass(qkv, aux, gate_b, d=1, reverse=True, side=side("o"),
                            side_swish_half=True, **common)
    y_m, gmgs = _mlstm_pass(qkv, aux, gate_b, d=0, reverse=False, side=side("gm"),
                            extra=(h_bwd, oz, m_norm_g[li][None, :].astype(F32)), **common)

    cw, cbias = s_conv_w[li].astype(F32), s_conv_b[li][None, :].astype(F32)
    xl4 = xbc_l.reshape(batch, rows, GRID_W, conv_ch)
    xc3 = xbc_c.reshape(batch, ctx_len, conv_ch)
    scan_in = [_conv(xl4, xc3, cw, cbias, o_, n_, 512)
               for o_, n_ in ((0, s_inner), (s_inner, s_bc), (s_inner + s_bc, s_bc))]
    scan_in.append(_dt_layout(aux, 1, batch, rows, GRID_W, ctx_len))
    dtb = _pad_cols(s_dt_bias[li].reshape(1, -1).astype(F32), LANES)
    alog = _pad_cols(s_a_log[li].reshape(1, -1).astype(F32), LANES)
    skip = jnp.repeat(s_d[li].astype(F32), S_HEAD_DIM)[None, :]
    scan = dict(ncc=ncc, nlc=nlc, hpg=hpg)
    w_blk = (LANES, d_model)
    y_bwd, w_bm_b, w_bs_b, w_out_b = _ssd_pass(
        *scan_in, dtb, alog, col0=s_heads, reverse=True,
        casts=((w_bm[li], w_blk), (w_bs[li], w_blk), (w_out[li], w_blk)), **scan)
    y_tot = _ssd_pass(*scan_in, dtb, alog, col0=0, reverse=False, extra=(y_bwd, skip), **scan)
    y_s = _ssd_out(y_tot.reshape(batch, GRID_W, rows, s_inner),
                   oz.reshape(batch, rows, GRID_W, 2 * s_inner),
                   s_norm_g[li][None, :].astype(F32), gw).reshape(n_lat, s_inner)

    tm_lat = min(1024, n_lat)
    mix_pre = _merge(y_m, y_s, w_bm_b, w_bs_b, gmgs, tm_lat, 256)
    mix = _matmul(mix_pre, w_out_b, BF16, tm_lat, tn_big, name="out_proj")
    x2 = x.reshape(n_lat, d_model)
    x_new, h_mod = _resid_mid(mix, x2, g_n[1][None, :], g_n[2][None, :], mod3, batch, tm=256)

    act = _swiglu_up(h_mod, w_gate_b, w_up_b, min(2048, n_lat), 256)
    hf = _matmul(act, w_down_b, BF16, min(512, n_lat), 512, name="ffn_down")
    out = _resid_out(hf, x_new, g_n[3][None, :], mod3, batch, tm=256)
    return out.reshape(batch, seq, d_model)
```
